```python
import jax, jax.numpy as jnp
from jax import lax
import numpy as np

D_MODEL = 1024
BATCH = 4
SEQ = 8192
DEPTH = 1

MIX_WIDTH = D_MODEL
ATTN_HEAD_DIM = 64
ATTN_WIDTH = MIX_WIDTH // 2
N_ATTN_HEADS = ATTN_WIDTH // ATTN_HEAD_DIM
DILATED_PATTERNS = ((128, 1), (512, 4), (2048, 16))
ATTN_BLOCK = 128
RET_V_DIM = 128
RET_V_WIDTH = MIX_WIDTH - ATTN_WIDTH
N_RET_HEADS = RET_V_WIDTH // RET_V_DIM
RET_QK_DIM = RET_V_DIM // 2
RET_QK_WIDTH = N_RET_HEADS * RET_QK_DIM
RET_CHUNK = 128
ROPE_BASE = 10000.0
D_FF = ((8 * D_MODEL // 3 + 127) // 128) * 128
CONV_WIDTH = 3
NORM_EPS = 1e-6
MASK_VALUE = -1e30
IN_SPLITS = (ATTN_WIDTH, ATTN_WIDTH, ATTN_WIDTH,
             RET_QK_WIDTH, RET_QK_WIDTH, RET_V_WIDTH,
             RET_V_WIDTH)
IN_WIDTH = sum(IN_SPLITS)
SPLIT_POINTS = tuple(int(i) for i in np.cumsum(IN_SPLITS)[:-1])

kernel_name = "hybrid_dilated_attn_retention_convffn"


def rms_norm(x, gain):
    xf = x.astype(jnp.float32)
    y = xf * lax.rsqrt(jnp.mean(xf * xf, axis=-1, keepdims=True) + NORM_EPS)
    return (y * gain.astype(jnp.float32)).astype(x.dtype)


def dilated_window_attention(q, k, v, window, dilation):
    b, s, h, e = q.shape
    steps = window // dilation
    assert steps <= ATTN_BLOCK
    sub_len = s // dilation
    nb = -(-sub_len // ATTN_BLOCK)
    sub_pad = nb * ATTN_BLOCK

    def to_sub(t):
        t = t.reshape(b, sub_len, dilation, h, e).transpose(0, 2, 1, 3, 4)
        return jnp.pad(t, ((0, 0), (0, 0), (0, sub_pad - sub_len), (0, 0), (0, 0)))

    def with_prev(t):
        tb = t.reshape(b, dilation, nb, ATTN_BLOCK, h, e)
        prev = jnp.pad(tb, ((0, 0), (0, 0), (1, 0), (0, 0), (0, 0), (0, 0)))[:, :, :-1]
        return jnp.concatenate([prev, tb], axis=3)

    qb = to_sub(q * (e ** -0.5)).reshape(b, dilation, nb, ATTN_BLOCK, h, e)
    kb = with_prev(to_sub(k))
    vb = with_prev(to_sub(v))
    scores = jnp.einsum('brnqhe,brnkhe->brnhqk', qb, kb).astype(jnp.float32)
    o_idx = jnp.arange(ATTN_BLOCK)[:, None]
    j_idx = jnp.arange(2 * ATTN_BLOCK)[None, :]
    dist = o_idx + ATTN_BLOCK - j_idx
    band = (dist >= 0) & (dist <= steps)
    key_pos = jnp.arange(nb)[:, None, None] * ATTN_BLOCK + j_idx[None] - ATTN_BLOCK
    mask = band[None] & (key_pos >= 0)
    scores = jnp.where(mask[None, None, :, None], scores, MASK_VALUE)
    lse = jax.nn.logsumexp(scores, axis=-1)
    probs = jnp.exp(scores - lse[..., None])
    out = jnp.einsum('brnhqk,brnkhe->brnqhe', probs.astype(v.dtype), vb)
    out = out.reshape(b, dilation, sub_pad, h, e)[:, :, :sub_len]
    out = out.transpose(0, 2, 1, 3, 4).reshape(b, s, h, e)
    lse = lse.transpose(0, 1, 2, 4, 3).reshape(b, dilation, sub_pad, h)[:, :, :sub_len]
    lse = lse.transpose(0, 2, 1, 3).reshape(b, s, h)
    return out, lse


def longnet_attention(q, k, v):
    outs, lses = [], []
    for window, dilation in DILATED_PATTERNS:
        o, l = dilated_window_attention(q, k, v, window, dilation)
        outs.append(o)
        lses.append(l)
    weights = jax.nn.softmax(jnp.stack(lses), axis=0)
    out = jnp.einsum('pbsh,pbshe->bshe', weights, jnp.stack(outs).astype(jnp.float32))
    return out.astype(q.dtype)


def rotary(x, positions):
    e = x.shape[-1]
    freqs = ROPE_BASE ** (-jnp.arange(0, e, 2, dtype=jnp.float32) / e)
    ang = positions.astype(jnp.float32)[:, None] * freqs[None]
    cos = jnp.cos(ang)[None, :, None, :]
    sin = jnp.sin(ang)[None, :, None, :]
    x1, x2 = x[..., : e // 2], x[..., e // 2:]
    return jnp.concatenate([x1 * cos - x2 * sin, x1 * sin + x2 * cos], axis=-1)


def chunkwise_retention(q, k, v):
    b, s, h, dk = q.shape
    dv = v.shape[-1]
    n = s // RET_CHUNK
    log_gamma = jnp.log1p(-jnp.exp2(-5.0 - jnp.arange(h, dtype=jnp.float32)))
    idx = jnp.arange(RET_CHUNK, dtype=jnp.float32)
    rel = idx[:, None] - idx[None, :]
    inner_decay = jnp.where(rel >= 0, jnp.exp(log_gamma[:, None, None] * jnp.maximum(rel, 0.0)), 0.0)
    qc = q.reshape(b, n, RET_CHUNK, h, dk)
    kc = k.reshape(b, n, RET_CHUNK, h, dk)
    vc = v.reshape(b, n, RET_CHUNK, h, dv)
    scores = jnp.einsum('bnihd,bnjhd->bnhij', qc, kc) * inner_decay
    inner = jnp.einsum('bnhij,bnjhe->bnihe', scores, vc)
    k_decay = jnp.exp(log_gamma[None, :] * (RET_CHUNK - 1 - idx)[:, None])
    chunk_kv = jnp.einsum('bnjhd,bnjhe->nbhde', kc * k_decay[:, :, None], vc)
    chunk_decay = jnp.exp(log_gamma * RET_CHUNK)[None, :, None, None]

    def step(state, kv):
        return chunk_decay * state + kv, state

    _, prev_states = lax.scan(step, jnp.zeros((b, h, dk, dv), jnp.float32), chunk_kv)
    q_decay = jnp.exp(log_gamma[None, :] * (idx + 1.0)[:, None])
    cross = jnp.einsum('bnihd,nbhde->bnihe', qc * q_decay[:, :, None], prev_states)
    return (inner + cross).reshape(b, s, h, dv)


def hybrid_mixer(h, w_in, w_out):
    b, s, _ = h.shape
    proj = h @ w_in
    q_a, k_a, v_a, q_r, k_r, v_r, g_r = jnp.split(proj, SPLIT_POINTS, axis=-1)
    heads_a = lambda t: t.reshape(b, s, N_ATTN_HEADS, ATTN_HEAD_DIM)
    attn = longnet_attention(heads_a(q_a), heads_a(k_a), heads_a(v_a)).reshape(b, s, ATTN_WIDTH)
    positions = jnp.arange(s)
    qr = rotary(q_r.astype(jnp.float32).reshape(b, s, N_RET_HEADS, RET_QK_DIM), positions)
    kr = rotary(k_r.astype(jnp.float32).reshape(b, s, N_RET_HEADS, RET_QK_DIM), positions) * (RET_QK_DIM ** -0.5)
    vr = v_r.astype(jnp.float32).reshape(b, s, N_RET_HEADS, RET_V_DIM)
    ret = chunkwise_retention(qr, kr, vr)
    ret = ret * lax.rsqrt(jnp.mean(ret * ret, axis=-1, keepdims=True) + NORM_EPS)
    ret = ret.reshape(b, s, RET_V_WIDTH) * jax.nn.silu(g_r.astype(jnp.float32))
    mixed = jnp.concatenate([attn, ret.astype(h.dtype)], axis=-1)
    return mixed @ w_out


def shift_right(t, n):
    return jnp.pad(t, ((0, 0), (n, 0), (0, 0)))[:, : t.shape[1]]


def conv_ffn(h, w_up, conv_w, conv_b, w_down):
    u = h @ w_up
    y = conv_b + conv_w[CONV_WIDTH - 1] * u
    for tap in range(1, CONV_WIDTH):
        y = y + conv_w[CONV_WIDTH - 1 - tap] * shift_right(u, tap)
    gate, up = jnp.split(y, 2, axis=-1)
    return (jax.nn.gelu(gate, approximate=True) * up) @ w_down


def setup_inputs(seed: int = 0) -> dict:
    key = jax.random.key(seed)
    ks = jax.random.split(key, 11)
    f32 = jnp.float32

    def gain(k):
        return 1.0 + 0.05 * jax.random.normal(k, (DEPTH, D_MODEL), f32)

    return {
        "x": jax.random.normal(ks[0], (BATCH, SEQ, D_MODEL), f32),
        "mix_pre_gain": gain(ks[1]),
        "mix_post_gain": gain(ks[2]),
        "w_in": jax.random.normal(ks[3], (DEPTH, D_MODEL, IN_WIDTH), f32) * D_MODEL ** -0.5,
        "w_out": jax.random.normal(ks[4], (DEPTH, MIX_WIDTH, D_MODEL), f32) * MIX_WIDTH ** -0.5,
        "ffn_pre_gain": gain(ks[5]),
        "ffn_post_gain": gain(ks[6]),
        "w_up": jax.random.normal(ks[7], (DEPTH, D_MODEL, 2 * D_FF), f32) * D_MODEL ** -0.5,
        "conv_w": jax.random.normal(ks[8], (DEPTH, CONV_WIDTH, 2 * D_FF), f32) * CONV_WIDTH ** -0.5,
        "conv_b": 0.02 * jax.random.normal(ks[9], (DEPTH, 2 * D_FF), f32),
        "w_down": jax.random.normal(ks[10], (DEPTH, D_FF, D_MODEL), f32) * D_FF ** -0.5,
    }


def reference(x, mix_pre_gain, mix_post_gain, w_in, w_out, ffn_pre_gain, ffn_post_gain,
              w_up, conv_w, conv_b, w_down):
    for layer in range(DEPTH):
        mix = hybrid_mixer(rms_norm(x, mix_pre_gain[layer]), w_in[layer], w_out[layer])
        x = x + rms_norm(mix, mix_post_gain[layer])
        ffn = conv_ffn(rms_norm(x, ffn_pre_gain[layer]), w_up[layer], conv_w[layer],
                       conv_b[layer], w_down[layer])
        x = x + rms_norm(ffn, ffn_post_gain[layer])
    return x
```

```python
import numpy as np
import jax
import jax.numpy as jnp
from jax import lax
from jax.experimental import pallas as pl
from jax.experimental.pallas import tpu as pltpu

F32 = jnp.float32
BF16 = jnp.bfloat16

ATTN_HEAD_DIM = 64
N_ATTN_HEADS = 8
ATTN_WIDTH = N_ATTN_HEADS * ATTN_HEAD_DIM
ATTN_STEPS = 128
DILATIONS = (1, 4, 16)
RET_V_DIM = 128
N_RET_HEADS = 4
RET_QK_DIM = 64
RET_QK_WIDTH = N_RET_HEADS * RET_QK_DIM
RET_V_WIDTH = N_RET_HEADS * RET_V_DIM
ROPE_BASE = 10000.0
CONV_WIDTH = 3
NORM_EPS = 1e-6
MASK_VALUE = -1e30

LANES = 128
SUBLANES = 8
HEAD_PAIR = LANES // ATTN_HEAD_DIM
N_PAIRS = N_ATTN_HEADS // HEAD_PAIR
MAX_DIL = max(DILATIONS)
ATTN_ROWS = 128
TOKEN_TILE = 512
RET_CHUNK = 256
FF_CHUNK = 256
VMEM_LIMIT_BYTES = 56 * 1024 * 1024


def _rms(x, gain):
    return x * lax.rsqrt(jnp.mean(x * x, axis=-1, keepdims=True) + NORM_EPS) * gain


def _inproj_retention_kernel(x_ref, gain_ref, w_ref, cos_ref, sin_ref, dmat_ref, qdec_ref, kdec_ref, sdec_ref,
                             qa_ref, ka_ref, va_ref, ret_ref, state_ref):
    @pl.when(pl.program_id(1) == 0)
    def _():
        state_ref[...] = jnp.zeros_like(state_ref)

    h = _rms(x_ref[...], gain_ref[...]).astype(BF16)
    proj = jnp.dot(h, w_ref[...], preferred_element_type=F32)

    for hp in range(N_PAIRS):
        cols = slice(hp * LANES, (hp + 1) * LANES)
        qa_ref[hp] = proj[:, cols].astype(BF16)
        ka_ref[hp] = proj[:, ATTN_WIDTH + hp * LANES:ATTN_WIDTH + (hp + 1) * LANES].astype(BF16)
        va_ref[hp] = proj[:, 2 * ATTN_WIDTH + hp * LANES:2 * ATTN_WIDTH + (hp + 1) * LANES].astype(BF16)

    base = 3 * ATTN_WIDTH
    lane = lax.broadcasted_iota(jnp.int32, (1, RET_QK_WIDTH), 1)
    first_half = (lane % RET_QK_DIM) < (RET_QK_DIM // 2)

    def rotary(t):
        swapped = jnp.where(first_half,
                            pltpu.roll(t, RET_QK_WIDTH - RET_QK_DIM // 2, axis=1),
                            pltpu.roll(t, RET_QK_DIM // 2, axis=1))
        return t * cos_ref[...] + swapped * sin_ref[...]

    qr = rotary(proj[:, base:base + RET_QK_WIDTH])
    kr = rotary(proj[:, base + RET_QK_WIDTH:base + 2 * RET_QK_WIDTH])
    vbase = base + 2 * RET_QK_WIDTH
    gbase = vbase + RET_V_WIDTH

    lane128 = lax.broadcasted_iota(jnp.int32, (1, LANES), 1)
    low_lanes = lane128 < RET_QK_DIM
    row128 = lax.broadcasted_iota(jnp.int32, (LANES, 1), 0)
    low_rows = row128 < RET_QK_DIM
    tm = x_ref.shape[0]
    nt = (((1,), (1,)), ((), ()))
    tn = (((0,), (0,)), ((), ()))

    for c in range(tm // RET_CHUNK):
        rows = slice(c * RET_CHUNK, (c + 1) * RET_CHUNK)
        for p in range(N_RET_HEADS // 2):
            cols = slice(p * LANES, (p + 1) * LANES)
            q2 = qr[rows, cols]
            k2 = kr[rows, cols]
            q_heads = (jnp.where(low_lanes, q2, 0.0), jnp.where(low_lanes, 0.0, q2))
            lhs = jnp.concatenate(q_heads, axis=0).astype(BF16)
            scores = lax.dot_general(lhs, k2.astype(BF16), nt, preferred_element_type=F32)
            kdec = (k2 * kdec_ref[:, cols]).astype(BF16)
            state = state_ref[p]
            state_bf = state.astype(BF16)
            kv = []
            for hh in range(2):
                head = 2 * p + hh
                v_h = proj[rows, vbase + head * RET_V_DIM:vbase + (head + 1) * RET_V_DIM].astype(BF16)
                s_h = (scores[hh * RET_CHUNK:(hh + 1) * RET_CHUNK] * dmat_ref[head]).astype(BF16)
                inner = jnp.dot(s_h, v_h, preferred_element_type=F32)
                q_dec = (q_heads[hh] * qdec_ref[:, cols]).astype(BF16)
                cross = jnp.dot(q_dec, state_bf, preferred_element_type=F32)
                o = inner + cross
                o = o * lax.rsqrt(jnp.mean(o * o, axis=-1, keepdims=True) + NORM_EPS)
                g = proj[rows, gbase + head * RET_V_DIM:gbase + (head + 1) * RET_V_DIM]
                o = o * (g * (1.0 / (1.0 + jnp.exp(-g))))
                ret_ref[rows, head * RET_V_DIM:(head + 1) * RET_V_DIM] = o.astype(BF16)
                kv.append(lax.dot_general(kdec, v_h, tn, preferred_element_type=F32))
            state_ref[p] = sdec_ref[p] * state + jnp.where(low_rows, kv[0], kv[1])


def _retention_constants(seq):
    freqs = ROPE_BASE ** (-np.arange(0, RET_QK_DIM, 2, dtype=np.float64) / RET_QK_DIM)
    ang = np.arange(seq, dtype=np.float64)[:, None] * freqs[None]
    cos_h = np.concatenate([np.cos(ang), np.cos(ang)], axis=1)
    sin_h = np.concatenate([-np.sin(ang), np.sin(ang)], axis=1)
    cos_t = np.tile(cos_h, (1, N_RET_HEADS)).astype(np.float32)
    sin_t = np.tile(sin_h, (1, N_RET_HEADS)).astype(np.float32)
    log_gamma = np.log1p(-np.exp2(-5.0 - np.arange(N_RET_HEADS, dtype=np.float64)))
    idx = np.arange(RET_CHUNK, dtype=np.float64)
    rel = idx[:, None] - idx[None, :]
    dmat = np.where(rel >= 0, np.exp(log_gamma[:, None, None] * np.maximum(rel, 0.0)), 0.0)
    qdec = np.exp(log_gamma[None, :] * (idx + 1.0)[:, None])
    kdec = np.exp(log_gamma[None, :] * (RET_CHUNK - 1 - idx)[:, None])
    qdec = np.repeat(qdec, RET_QK_DIM, axis=1)
    kdec = np.repeat(kdec, RET_QK_DIM, axis=1)
    sdec = np.repeat(np.exp(log_gamma * RET_CHUNK), RET_QK_DIM).reshape(N_RET_HEADS // 2, LANES, 1)
    sdec = np.broadcast_to(sdec, (N_RET_HEADS // 2, LANES, LANES))
    f = lambda a: jnp.asarray(np.ascontiguousarray(a), dtype=F32)
    return f(cos_t), f(sin_t), f(dmat), f(qdec), f(kdec), f(sdec)


def _inproj_retention(x, gain, w_in_bf):
    b, s, d = x.shape
    tm = TOKEN_TILE
    in_width = w_in_bf.shape[1]
    cos_t, sin_t, dmat, qdec, kdec, sdec = _retention_constants(s)
    const = lambda shape: pl.BlockSpec(shape, lambda bi, i: (0,) * len(shape))
    pair_spec = pl.BlockSpec((None, N_PAIRS, tm, LANES), lambda bi, i: (bi, 0, i, 0))
    pair_shape = jax.ShapeDtypeStruct((b, N_PAIRS, s, LANES), BF16)
    return pl.pallas_call(
        _inproj_retention_kernel,
        grid=(b, s // tm),
        in_specs=[
            pl.BlockSpec((None, tm, d), lambda bi, i: (bi, i, 0)),
            const((1, d)),
            const((d, in_width)),
            pl.BlockSpec((tm, RET_QK_WIDTH), lambda bi, i: (i, 0)),
            pl.BlockSpec((tm, RET_QK_WIDTH), lambda bi, i: (i, 0)),
            const((N_RET_HEADS, RET_CHUNK, RET_CHUNK)),
            const((RET_CHUNK, RET_QK_WIDTH)),
            const((RET_CHUNK, RET_QK_WIDTH)),
            const((N_RET_HEADS // 2, LANES, LANES)),
        ],
        out_specs=[pair_spec, pair_spec, pair_spec,
                   pl.BlockSpec((None, tm, RET_V_WIDTH), lambda bi, i: (bi, i, 0))],
        out_shape=[pair_shape, pair_shape, pair_shape, jax.ShapeDtypeStruct((b, s, RET_V_WIDTH), BF16)],
        scratch_shapes=[pltpu.VMEM((N_RET_HEADS // 2, LANES, LANES), F32)],
        compiler_params=pltpu.CompilerParams(dimension_semantics=("arbitrary", "arbitrary"),
                                             vmem_limit_bytes=VMEM_LIMIT_BYTES),
        name="inproj_retention",
    )(x, gain, w_in_bf, cos_t, sin_t, dmat, qdec, kdec, sdec)


def _attention_biases():
    q = np.arange(ATTN_ROWS)[:, None]
    k = np.arange(2 * ATTN_ROWS)[None, :]
    out = []
    dist = ATTN_ROWS + q - k
    out += [(dist, k >= ATTN_ROWS)]
    dist = 4 * (32 + q % 32 - k % 64) + q // 32 - k // 64
    out += [(dist, k % 64 >= 32)]
    dist = 16 * (8 + q % 8 - k % 16) + q // 8 - k // 16
    out += [(dist, k % 16 >= 8)]
    biases = []
    for dist, in_seq in out:
        band = (dist >= 0) & (dist <= ATTN_STEPS)
        biases.append(np.where(band, 0.0, MASK_VALUE))
        biases.append(np.where(band & in_seq, 0.0, MASK_VALUE))
    return jnp.asarray(np.stack(biases), dtype=F32)


def _attention_kernel(bias_ref, q_ref, k_ref, v_ref, o_ref, q0_s, q1_s, k_s, v_s, out_s, lse_s):
    first_tile = pl.program_id(2) == 0
    rows = ATTN_ROWS

    @pl.when(first_tile)
    def _():
        k_s[:, 0:rows, :] = jnp.zeros((MAX_DIL, rows, LANES), F32)
        v_s[:, 0:rows, :] = jnp.zeros((MAX_DIL, rows, LANES), F32)

    lane = lax.broadcasted_iota(jnp.int32, (1, LANES), 1)
    low = lane < ATTN_HEAD_DIM
    for r in range(MAX_DIL):
        cols = slice(r * LANES, (r + 1) * LANES)
        qf = q_ref[:, cols].astype(F32)
        q0_s[r] = jnp.where(low, qf, 0.0)
        q1_s[r] = jnp.where(low, 0.0, qf)
        k_s[r, rows:2 * rows, :] = k_ref[:, cols].astype(F32)
        v_s[r, rows:2 * rows, :] = v_ref[:, cols].astype(F32)

    ones = jnp.ones((2 * rows, LANES), BF16)
    nt = (((1,), (1,)), ((), ()))

    def attend(q0_pieces, q1_pieces, k_pieces, v_pieces, bias):
        lhs = jnp.concatenate(q0_pieces + q1_pieces, axis=0).astype(BF16)
        k2 = jnp.concatenate(k_pieces, axis=0).astype(BF16)
        v2 = jnp.concatenate(v_pieces, axis=0).astype(BF16)
        s = lax.dot_general(lhs, k2, nt, preferred_element_type=F32)
        s = s + jnp.concatenate([bias, bias], axis=0)
        m = jnp.max(s, axis=-1, keepdims=True)
        p = jnp.exp(s - m).astype(BF16)
        pv = jnp.dot(p, jnp.concatenate([v2, ones], axis=1), preferred_element_type=F32)
        denom = pv[:, LANES:]
        out = pv[:, :LANES] * (1.0 / denom)
        lse = m + jnp.log(denom)
        return (jnp.where(low, out[:rows], out[rows:]), jnp.where(low, lse[:rows], lse[rows:]))

    def variant(cond):
        return jnp.where(cond, 1, 0)

    def dil16(r, carry):
        out, lse = attend([q0_s[r]], [q1_s[r]], [k_s[r]], [v_s[r]], bias_ref[variant(first_tile)])
        out_s[0, r] = out
        lse_s[0, r] = lse
        return carry

    lax.fori_loop(0, MAX_DIL, dil16, 0)

    def dil4(idx, carry):
        c = idx // 4
        j = idx % 4
        qrow = pl.ds(pl.multiple_of(32 * j, 32), 32)
        krow = pl.ds(pl.multiple_of(96 + 32 * j, 32), 64)
        out, lse = attend([q0_s[4 * a + c, qrow, :] for a in range(4)],
                          [q1_s[4 * a + c, qrow, :] for a in range(4)],
                          [k_s[4 * a + c, krow, :] for a in range(4)],
                          [v_s[4 * a + c, krow, :] for a in range(4)],
                          bias_ref[2 + variant(first_tile & (j == 0))])
        for a in range(4):
            out_s[1, 4 * a + c, qrow, :] = out[32 * a:32 * (a + 1)]
            lse_s[1, 4 * a + c, qrow, :] = lse[32 * a:32 * (a + 1)]
        return carry

    lax.fori_loop(0, 16, dil4, 0)

    def dil1(j, carry):
        qrow = pl.ds(pl.multiple_of(8 * j, 8), 8)
        krow = pl.ds(pl.multiple_of(120 + 8 * j, 8), 16)
        out, lse = attend([q0_s[r, qrow, :] for r in range(MAX_DIL)],
                          [q1_s[r, qrow, :] for r in range(MAX_DIL)],
                          [k_s[r, krow, :] for r in range(MAX_DIL)],
                          [v_s[r, krow, :] for r in range(MAX_DIL)],
                          bias_ref[4 + variant(first_tile & (j == 0))])
        for r in range(MAX_DIL):
            out_s[2, r, qrow, :] = out[8 * r:8 * (r + 1)]
            lse_s[2, r, qrow, :] = lse[8 * r:8 * (r + 1)]
        return carry

    lax.fori_loop(0, 16, dil1, 0)

    for r in range(MAX_DIL):
        l0, l1, l2 = lse_s[0, r], lse_s[1, r], lse_s[2, r]
        top = jnp.maximum(jnp.maximum(l0, l1), l2)
        e0, e1, e2 = jnp.exp(l0 - top), jnp.exp(l1 - top), jnp.exp(l2 - top)
        mixed = (e0 * out_s[0, r] + e1 * out_s[1, r] + e2 * out_s[2, r]) * (1.0 / (e0 + e1 + e2))
        o_ref[:, r * LANES:(r + 1) * LANES] = mixed.astype(BF16)
        k_s[r, 0:rows, :] = k_s[r, rows:2 * rows, :]
        v_s[r, 0:rows, :] = v_s[r, rows:2 * rows, :]


def _dilated_attention(qa, ka, va):
    b, n_pairs, s, _ = qa.shape
    tile_tokens = ATTN_ROWS * MAX_DIL
    width = MAX_DIL * LANES
    view = lambda t: t.reshape(b, n_pairs, s // MAX_DIL, width)
    tile = pl.BlockSpec((None, None, ATTN_ROWS, width), lambda bi, hp, i: (bi, hp, i, 0))
    biases = _attention_biases()
    scratch = lambda n: pltpu.VMEM((MAX_DIL, n, LANES), F32)
    out = pl.pallas_call(
        _attention_kernel,
        grid=(b, n_pairs, s // tile_tokens),
        in_specs=[pl.BlockSpec(biases.shape, lambda bi, hp, i: (0, 0, 0)), tile, tile, tile],
        out_specs=tile,
        out_shape=jax.ShapeDtypeStruct((b, n_pairs, s // MAX_DIL, width), BF16),
        scratch_shapes=[scratch(ATTN_ROWS), scratch(ATTN_ROWS), scratch(2 * ATTN_ROWS), scratch(2 * ATTN_ROWS),
                        pltpu.VMEM((len(DILATIONS), MAX_DIL, ATTN_ROWS, LANES), F32),
                        pltpu.VMEM((len(DILATIONS), MAX_DIL, ATTN_ROWS, LANES), F32)],
        compiler_params=pltpu.CompilerParams(dimension_semantics=("arbitrary", "arbitrary", "arbitrary"),
                                             vmem_limit_bytes=VMEM_LIMIT_BYTES),
        name="dilated_attention",
    )(biases, view(qa), view(ka), view(va))
    return out.reshape(b, n_pairs, s, LANES)


def _outproj_ffn_kernel(x_ref, attn_ref, ret_ref, gpost_ref, gpre_ref, gffn_ref, wout_ref, wup_ref, cw_ref, cb_ref,
                        wdown_ref, o_ref, act_s, u_s, halo_s):
    tm = x_ref.shape[0]
    n_chunks = act_s.shape[1] // FF_CHUNK
    width = 2 * FF_CHUNK

    @pl.when(pl.program_id(1) == 0)
    def _():
        halo_s[...] = jnp.zeros_like(halo_s)

    mixed = jnp.concatenate([attn_ref[hp] for hp in range(N_PAIRS)] + [ret_ref[...]], axis=-1)
    mix = jnp.dot(mixed, wout_ref[...], preferred_element_type=F32)
    x1 = x_ref[...] + _rms(mix, gpost_ref[...])
    hn = _rms(x1, gpre_ref[...]).astype(BF16)

    for c in range(n_chunks):
        cols = slice(c * width, (c + 1) * width)
        u = jnp.dot(hn, wup_ref[:, cols], preferred_element_type=F32)
        u_s[0:SUBLANES, :] = halo_s[:, cols]
        u_s[SUBLANES:SUBLANES + tm, :] = u
        halo_s[:, cols] = u[tm - SUBLANES:tm]
        y = (cb_ref[:, cols] + cw_ref[2:3, cols] * u
             + cw_ref[1:2, cols] * u_s[SUBLANES - 1:SUBLANES - 1 + tm, :]
             + cw_ref[0:1, cols] * u_s[SUBLANES - 2:SUBLANES - 2 + tm, :])
        gate = y[:, :FF_CHUNK]
        cdf = 0.5 * (1.0 + jnp.tanh(np.sqrt(2.0 / np.pi).astype(np.float32) * (gate + 0.044715 * (gate * gate * gate))))
        act_s[:, c * FF_CHUNK:(c + 1) * FF_CHUNK] = (gate * cdf * y[:, FF_CHUNK:]).astype(BF16)

    ffn = jnp.dot(act_s[...], wdown_ref[...], preferred_element_type=F32)
    o_ref[...] = x1 + _rms(ffn, gffn_ref[...])


def _outproj_ffn(x, attn, ret, g_post, g_pre, g_ffn, w_out_bf, w_up_bf, conv_w, conv_b, w_down_bf):
    b, s, d = x.shape
    tm = TOKEN_TILE
    d_ff = w_down_bf.shape[0]
    const = lambda shape: pl.BlockSpec(shape, lambda bi, i: (0,) * len(shape), pipeline_mode=pl.Buffered(1))
    return pl.pallas_call(
        _outproj_ffn_kernel,
        grid=(b, s // tm),
        in_specs=[
            pl.BlockSpec((None, tm, d), lambda bi, i: (bi, i, 0)),
            pl.BlockSpec((None, N_PAIRS, tm, LANES), lambda bi, i: (bi, 0, i, 0)),
            pl.BlockSpec((None, tm, RET_V_WIDTH), lambda bi, i: (bi, i, 0)),
            const((1, d)), const((1, d)), const((1, d)),
            const(w_out_bf.shape), const(w_up_bf.shape), const(conv_w.shape), const(conv_b.shape),
            const(w_down_bf.shape),
        ],
        out_specs=pl.BlockSpec((None, tm, d), lambda bi, i: (bi, i, 0)),
        out_shape=jax.ShapeDtypeStruct((b, s, d), F32),
        scratch_shapes=[pltpu.VMEM((tm, d_ff), BF16),
                        pltpu.VMEM((tm + SUBLANES, 2 * FF_CHUNK), F32),
                        pltpu.VMEM((SUBLANES, 2 * d_ff), F32)],
        compiler_params=pltpu.CompilerParams(dimension_semantics=("arbitrary", "arbitrary"),
                                             vmem_limit_bytes=VMEM_LIMIT_BYTES),
        name="outproj_ffn",
    )(x, attn, ret, g_post, g_pre, g_ffn, w_out_bf, w_up_bf, conv_w, conv_b, w_down_bf)


def _interleave_gate_up(t, d_ff):
    lead = t.shape[:-1]
    g = t[..., :d_ff].reshape(*lead, d_ff // FF_CHUNK, FF_CHUNK)
    u = t[..., d_ff:].reshape(*lead, d_ff // FF_CHUNK, FF_CHUNK)
    return jnp.concatenate([g, u], axis=-1).reshape(*lead, 2 * d_ff)


def _layer(x, g_mix_pre, g_mix_post, w_in, w_out, g_ffn_pre, g_ffn_post, w_up, conv_w, conv_b, w_down):
    d = x.shape[-1]
    d_ff = w_down.shape[0]
    col_scale = np.ones((w_in.shape[1],), np.float32)
    col_scale[:ATTN_WIDTH] = ATTN_HEAD_DIM ** -0.5
    kr0 = 3 * ATTN_WIDTH + RET_QK_WIDTH
    col_scale[kr0:kr0 + RET_QK_WIDTH] = RET_QK_DIM ** -0.5
    w_in_bf = (w_in * col_scale).astype(BF16)
    qa, ka, va, ret = _inproj_retention(x, g_mix_pre.reshape(1, d), w_in_bf)
    attn = _dilated_attention(qa, ka, va)
    return _outproj_ffn(x, attn, ret, g_mix_post.reshape(1, d), g_ffn_pre.reshape(1, d), g_ffn_post.reshape(1, d),
                        w_out.astype(BF16), _interleave_gate_up(w_up, d_ff).astype(BF16),
                        _interleave_gate_up(conv_w, d_ff), _interleave_gate_up(conv_b.reshape(1, -1), d_ff),
                        w_down.astype(BF16))


def kernel(x, mix_pre_gain, mix_post_gain, w_in, w_out, ffn_pre_gain, ffn_post_gain, w_up, conv_w, conv_b, w_down):
    for layer in range(mix_pre_gain.shape[0]):
        x = _layer(x, mix_pre_gain[layer], mix_post_gain[layer], w_in[layer], w_out[layer], ffn_pre_gain[layer],
                   ffn_post_gain[layer], w_up[layer], conv_w[layer], conv_b[layer], w_down[layer])
    return x
```

```python
import numpy as np
import jax
import jax.numpy as jnp
from jax import lax
from jax.experimental import pallas as pl
from jax.experimental.pallas import tpu as pltpu

F32 = jnp.float32
BF16 = jnp.bfloat16

ATTN_HEAD_DIM = 64
N_ATTN_HEADS = 8
ATTN_WIDTH = N_ATTN_HEADS * ATTN_HEAD_DIM
ATTN_STEPS = 128
DILATIONS = (1, 4, 16)
RET_V_DIM = 128
N_RET_HEADS = 4
RET_QK_DIM = 64
RET_QK_WIDTH = N_RET_HEADS * RET_QK_DIM
RET_V_WIDTH = N_RET_HEADS * RET_V_DIM
ROPE_BASE = 10000.0
CONV_WIDTH = 3
NORM_EPS = 1e-6
MASK_VALUE = -1e30

LANES = 128
SUBLANES = 8
HEAD_PAIR = LANES // ATTN_HEAD_DIM
N_PAIRS = N_ATTN_HEADS // HEAD_PAIR
MAX_DIL = max(DILATIONS)
ATTN_ROWS = 128
TOKEN_TILE = 512
RET_CHUNK = 256
FF_CHUNK = 256
ATTN_UNROLL = 16
VMEM_LIMIT_BYTES = 56 * 1024 * 1024


def _rms(x, gain):
    return x * lax.rsqrt(jnp.mean(x * x, axis=-1, keepdims=True) + NORM_EPS) * gain


def _inproj_retention_kernel(x_ref, gain_ref, w_ref, cos_ref, sin_ref, dmat_ref, qdec_ref, kdec_ref, sdec_ref,
                             qa_ref, ka_ref, va_ref, ret_ref, state_ref, stage_ref):
    @pl.when(pl.program_id(1) == 0)
    def _():
        state_ref[...] = jnp.zeros_like(state_ref)

    h = _rms(x_ref[...], gain_ref[...]).astype(BF16)
    proj = jnp.dot(h, w_ref[...], preferred_element_type=F32)

    view_rows = x_ref.shape[0] // MAX_DIL
    for t, out_ref in enumerate((qa_ref, ka_ref, va_ref)):
        for hp in range(N_PAIRS):
            col = t * N_PAIRS + hp
            stage_ref[col] = proj[:, col * LANES:(col + 1) * LANES]
            for r in range(MAX_DIL):
                piece = stage_ref[col, pl.ds(r, view_rows, stride=MAX_DIL), :]
                out_ref[hp, :, r * LANES:(r + 1) * LANES] = piece.astype(BF16)

    base = 3 * ATTN_WIDTH
    lane = lax.broadcasted_iota(jnp.int32, (1, RET_QK_WIDTH), 1)
    first_half = (lane % RET_QK_DIM) < (RET_QK_DIM // 2)

    def rotary(t):
        swapped = jnp.where(first_half,
                            pltpu.roll(t, RET_QK_WIDTH - RET_QK_DIM // 2, axis=1),
                            pltpu.roll(t, RET_QK_DIM // 2, axis=1))
        return t * cos_ref[...] + swapped * sin_ref[...]

    qr = rotary(proj[:, base:base + RET_QK_WIDTH])
    kr = rotary(proj[:, base + RET_QK_WIDTH:base + 2 * RET_QK_WIDTH])
    vbase = base + 2 * RET_QK_WIDTH
    gbase = vbase + RET_V_WIDTH

    lane128 = lax.broadcasted_iota(jnp.int32, (1, LANES), 1)
    low_lanes = lane128 < RET_QK_DIM
    row128 = lax.broadcasted_iota(jnp.int32, (LANES, 1), 0)
    low_rows = row128 < RET_QK_DIM
    tm = x_ref.shape[0]
    nt = (((1,), (1,)), ((), ()))
    tn = (((0,), (0,)), ((), ()))

    for c in range(tm // RET_CHUNK):
        rows = slice(c * RET_CHUNK, (c + 1) * RET_CHUNK)
        for p in range(N_RET_HEADS // 2):
            cols = slice(p * LANES, (p + 1) * LANES)
            q2 = qr[rows, cols]
            k2 = kr[rows, cols]
            q_heads = (jnp.where(low_lanes, q2, 0.0), jnp.where(low_lanes, 0.0, q2))
            lhs = jnp.concatenate(q_heads, axis=0).astype(BF16)
            scores = lax.dot_general(lhs, k2.astype(BF16), nt, preferred_element_type=F32)
            kdec = (k2 * kdec_ref[:, cols]).astype(BF16)
            state = state_ref[p]
            state_bf = state.astype(BF16)
            kv = []
            for hh in range(2):
                head = 2 * p + hh
                v_h = proj[rows, vbase + head * RET_V_DIM:vbase + (head + 1) * RET_V_DIM].astype(BF16)
                s_h = (scores[hh * RET_CHUNK:(hh + 1) * RET_CHUNK] * dmat_ref[head]).astype(BF16)
                inner = jnp.dot(s_h, v_h, preferred_element_type=F32)
                q_dec = (q_heads[hh] * qdec_ref[:, cols]).astype(BF16)
                cross = jnp.dot(q_dec, state_bf, preferred_element_type=F32)
                o = inner + cross
                o = o * lax.rsqrt(jnp.mean(o * o, axis=-1, keepdims=True) + NORM_EPS)
                g = proj[rows, gbase + head * RET_V_DIM:gbase + (head + 1) * RET_V_DIM]
                o = o * (g * (1.0 / (1.0 + jnp.exp(-g))))
                ret_ref[rows, head * RET_V_DIM:(head + 1) * RET_V_DIM] = o.astype(BF16)
                kv.append(lax.dot_general(kdec, v_h, tn, preferred_element_type=F32))
            state_ref[p] = sdec_ref[p] * state + jnp.where(low_rows, kv[0], kv[1])


def _retention_constants(seq):
    freqs = ROPE_BASE ** (-np.arange(0, RET_QK_DIM, 2, dtype=np.float64) / RET_QK_DIM)
    ang = np.arange(seq, dtype=np.float64)[:, None] * freqs[None]
    cos_h = np.concatenate([np.cos(ang), np.cos(ang)], axis=1)
    sin_h = np.concatenate([-np.sin(ang), np.sin(ang)], axis=1)
    cos_t = np.tile(cos_h, (1, N_RET_HEADS)).astype(np.float32)
    sin_t = np.tile(sin_h, (1, N_RET_HEADS)).astype(np.float32)
    log_gamma = np.log1p(-np.exp2(-5.0 - np.arange(N_RET_HEADS, dtype=np.float64)))
    idx = np.arange(RET_CHUNK, dtype=np.float64)
    rel = idx[:, None] - idx[None, :]
    dmat = np.where(rel >= 0, np.exp(log_gamma[:, None, None] * np.maximum(rel, 0.0)), 0.0)
    qdec = np.exp(log_gamma[None, :] * (idx + 1.0)[:, None])
    kdec = np.exp(log_gamma[None, :] * (RET_CHUNK - 1 - idx)[:, None])
    qdec = np.repeat(qdec, RET_QK_DIM, axis=1)
    kdec = np.repeat(kdec, RET_QK_DIM, axis=1)
    sdec = np.repeat(np.exp(log_gamma * RET_CHUNK), RET_QK_DIM).reshape(N_RET_HEADS // 2, LANES, 1)
    sdec = np.broadcast_to(sdec, (N_RET_HEADS // 2, LANES, LANES))
    f = lambda a: jnp.asarray(np.ascontiguousarray(a), dtype=F32)
    return f(cos_t), f(sin_t), f(dmat), f(qdec), f(kdec), f(sdec)


def _inproj_retention(x, gain, w_in_bf):
    b, s, d = x.shape
    tm = TOKEN_TILE
    in_width = w_in_bf.shape[1]
    cos_t, sin_t, dmat, qdec, kdec, sdec = _retention_constants(s)
    const = lambda shape: pl.BlockSpec(shape, lambda bi, i: (0,) * len(shape))
    pair_spec = pl.BlockSpec((None, N_PAIRS, tm // MAX_DIL, MAX_DIL * LANES), lambda bi, i: (bi, 0, i, 0))
    pair_shape = jax.ShapeDtypeStruct((b, N_PAIRS, s // MAX_DIL, MAX_DIL * LANES), BF16)
    return pl.pallas_call(
        _inproj_retention_kernel,
        grid=(b, s // tm),
        in_specs=[
            pl.BlockSpec((None, tm, d), lambda bi, i: (bi, i, 0)),
            const((1, d)),
            const((d, in_width)),
            pl.BlockSpec((tm, RET_QK_WIDTH), lambda bi, i: (i, 0)),
            pl.BlockSpec((tm, RET_QK_WIDTH), lambda bi, i: (i, 0)),
            const((N_RET_HEADS, RET_CHUNK, RET_CHUNK)),
            const((RET_CHUNK, RET_QK_WIDTH)),
            const((RET_CHUNK, RET_QK_WIDTH)),
            const((N_RET_HEADS // 2, LANES, LANES)),
        ],
        out_specs=[pair_spec, pair_spec, pair_spec,
                   pl.BlockSpec((None, tm, RET_V_WIDTH), lambda bi, i: (bi, i, 0))],
        out_shape=[pair_shape, pair_shape, pair_shape, jax.ShapeDtypeStruct((b, s, RET_V_WIDTH), BF16)],
        scratch_shapes=[pltpu.VMEM((N_RET_HEADS // 2, LANES, LANES), F32),
                        pltpu.VMEM((3 * N_PAIRS, tm, LANES), F32)],
        compiler_params=pltpu.CompilerParams(dimension_semantics=("arbitrary", "arbitrary"),
                                             vmem_limit_bytes=VMEM_LIMIT_BYTES),
        name="inproj_retention",
    )(x, gain, w_in_bf, cos_t, sin_t, dmat, qdec, kdec, sdec)


def _attention_biases():
    q = np.arange(ATTN_ROWS)[None, :]
    k = np.arange(2 * ATTN_ROWS)[:, None]
    out = []
    out += [(ATTN_ROWS + q - k, k >= ATTN_ROWS)]
    out += [(4 * (32 + q % 32 - k % 64) + q // 32 - k // 64, k % 64 >= 32)]
    out += [(16 * (8 + q % 8 - k % 16) + q // 8 - k // 16, k % 16 >= 8)]
    biases = []
    for dist, in_seq in out:
        band = (dist >= 0) & (dist <= ATTN_STEPS)
        biases.append(np.where(band, 0.0, MASK_VALUE))
        biases.append(np.where(band & in_seq, 0.0, MASK_VALUE))
    return jnp.asarray(np.stack(biases), dtype=BF16)


def _attention_kernel(bias_ref, q_ref, k_ref, v_ref, o_ref, q0_s, q1_s, k_s, v_s, num_s, den_s, max_s):
    first_tile = pl.program_id(2) == 0
    rows = ATTN_ROWS

    @pl.when(first_tile)
    def _():
        k_s[:, 0:rows, :] = jnp.zeros((MAX_DIL, rows, LANES), F32)
        v_s[:, 0:rows, :] = jnp.zeros((MAX_DIL, rows, LANES), F32)

    lane = lax.broadcasted_iota(jnp.int32, (1, LANES), 1)
    low = lane < ATTN_HEAD_DIM
    for r in range(MAX_DIL):
        cols = slice(r * LANES, (r + 1) * LANES)
        qf = q_ref[:, cols].astype(F32)
        q0_s[r] = jnp.where(low, qf, 0.0)
        q1_s[r] = jnp.where(low, 0.0, qf)
        k_s[r, rows:2 * rows, :] = k_ref[:, cols].astype(F32)
        v_s[r, rows:2 * rows, :] = v_ref[:, cols].astype(F32)

    ones = jnp.ones((2 * rows, LANES), BF16)
    pick = (lax.broadcasted_iota(jnp.int32, (2 * rows, LANES), 0) % rows
            == lax.broadcasted_iota(jnp.int32, (2 * rows, LANES), 1))
    pick = jnp.where(pick, 1.0, 0.0).astype(BF16)
    nt = (((1,), (1,)), ((), ()))

    def attend(q0_pieces, q1_pieces, k_pieces, v_pieces, bias):
        q2 = jnp.concatenate(q0_pieces + q1_pieces, axis=0).astype(BF16)
        k2 = jnp.concatenate(k_pieces, axis=0).astype(BF16)
        v2 = jnp.concatenate(v_pieces, axis=0).astype(BF16)
        s = lax.dot_general(jnp.concatenate([q2, pick], axis=1), jnp.concatenate([k2, bias], axis=1), nt,
                            preferred_element_type=F32)
        m = jnp.max(s, axis=-1, keepdims=True)
        p = jnp.exp2(s - m).astype(BF16)
        pv = jnp.dot(p, jnp.concatenate([v2, ones], axis=1), preferred_element_type=F32)
        return (jnp.where(low, pv[:rows, :LANES], pv[rows:, :LANES]),
                jnp.where(low, pv[:rows, LANES:], pv[rows:, LANES:]),
                jnp.where(low, m[:rows], m[rows:]))

    def variant(cond):
        return jnp.where(cond, 1, 0)

    def dil16(r, carry):
        num, den, mx = attend([q0_s[r]], [q1_s[r]], [k_s[r]], [v_s[r]], bias_ref[variant(first_tile)])
        num_s[0, r] = num
        den_s[0, r] = den
        max_s[0, r] = mx
        return carry

    lax.fori_loop(0, MAX_DIL, dil16, 0, unroll=ATTN_UNROLL)

    def dil4(idx, carry):
        c = idx // 4
        j = idx % 4
        qrow = pl.ds(pl.multiple_of(32 * j, 32), 32)
        krow = pl.ds(pl.multiple_of(96 + 32 * j, 32), 64)
        num, den, mx = attend([q0_s[4 * a + c, qrow, :] for a in range(4)],
                              [q1_s[4 * a + c, qrow, :] for a in range(4)],
                              [k_s[4 * a + c, krow, :] for a in range(4)],
                              [v_s[4 * a + c, krow, :] for a in range(4)],
                              bias_ref[2 + variant(first_tile & (j == 0))])
        for a in range(4):
            piece = slice(32 * a, 32 * (a + 1))
            num_s[1, 4 * a + c, qrow, :] = num[piece]
            den_s[1, 4 * a + c, qrow, :] = den[piece]
            max_s[1, 4 * a + c, qrow, :] = mx[piece]
        return carry

    lax.fori_loop(0, 16, dil4, 0, unroll=ATTN_UNROLL)

    def dil1(j, carry):
        qrow = pl.ds(pl.multiple_of(8 * j, 8), 8)
        krow = pl.ds(pl.multiple_of(120 + 8 * j, 8), 16)
        num, den, mx = attend([q0_s[r, qrow, :] for r in range(MAX_DIL)],
                              [q1_s[r, qrow, :] for r in range(MAX_DIL)],
                              [k_s[r, krow, :] for r in range(MAX_DIL)],
                              [v_s[r, krow, :] for r in range(MAX_DIL)],
                              bias_ref[4 + variant(first_tile & (j == 0))])
        for r in range(MAX_DIL):
            piece = slice(8 * r, 8 * (r + 1))
            num_s[2, r, qrow, :] = num[piece]
            den_s[2, r, qrow, :] = den[piece]
            max_s[2, r, qrow, :] = mx[piece]
        return carry

    lax.fori_loop(0, 16, dil1, 0, unroll=ATTN_UNROLL)

    for r in range(MAX_DIL):
        m0, m1, m2 = max_s[0, r], max_s[1, r], max_s[2, r]
        top = jnp.maximum(jnp.maximum(m0, m1), m2)
        e0, e1, e2 = jnp.exp2(m0 - top), jnp.exp2(m1 - top), jnp.exp2(m2 - top)
        num = e0 * num_s[0, r] + e1 * num_s[1, r] + e2 * num_s[2, r]
        den = e0 * den_s[0, r] + e1 * den_s[1, r] + e2 * den_s[2, r]
        o_ref[:, r * LANES:(r + 1) * LANES] = (num * (1.0 / den)).astype(BF16)
        k_s[r, 0:rows, :] = k_s[r, rows:2 * rows, :]
        v_s[r, 0:rows, :] = v_s[r, rows:2 * rows, :]


def _dilated_attention(qa, ka, va):
    b, n_pairs, view_len, width = qa.shape
    tile = pl.BlockSpec((None, None, ATTN_ROWS, width), lambda bi, hp, i: (bi, hp, i, 0))
    biases = _attention_biases()
    scratch = lambda n: pltpu.VMEM((MAX_DIL, n, LANES), F32)
    return pl.pallas_call(
        _attention_kernel,
        grid=(b, n_pairs, view_len // ATTN_ROWS),
        in_specs=[pl.BlockSpec(biases.shape, lambda bi, hp, i: (0, 0, 0)), tile, tile, tile],
        out_specs=tile,
        out_shape=jax.ShapeDtypeStruct(qa.shape, BF16),
        scratch_shapes=[scratch(ATTN_ROWS), scratch(ATTN_ROWS), scratch(2 * ATTN_ROWS), scratch(2 * ATTN_ROWS)]
        + [pltpu.VMEM((len(DILATIONS), MAX_DIL, ATTN_ROWS, LANES), F32)] * 3,
        compiler_params=pltpu.CompilerParams(dimension_semantics=("arbitrary", "arbitrary", "arbitrary"),
                                             vmem_limit_bytes=VMEM_LIMIT_BYTES),
        name="dilated_attention",
    )(biases, qa, ka, va)


def _outproj_ffn_kernel(x_ref, attn_ref, ret_ref, gpost_ref, gpre_ref, gffn_ref, wout_ref, wup_ref, cw_ref, cb_ref,
                        wdown_ref, o_ref, act_s, u_s, halo_s, stage_s):
    tm = x_ref.shape[0]
    n_chunks = act_s.shape[1] // FF_CHUNK
    width = 2 * FF_CHUNK

    @pl.when(pl.program_id(1) == 0)
    def _():
        halo_s[...] = jnp.zeros_like(halo_s)

    for hp in range(N_PAIRS):
        for r in range(MAX_DIL):
            piece = attn_ref[hp, :, r * LANES:(r + 1) * LANES].astype(F32)
            stage_s[hp, pl.ds(r, tm // MAX_DIL, stride=MAX_DIL), :] = piece
    mixed = jnp.concatenate([stage_s[hp].astype(BF16) for hp in range(N_PAIRS)] + [ret_ref[...]],
                            axis=-1)
    mix = jnp.dot(mixed, wout_ref[...], preferred_element_type=F32)
    x1 = x_ref[...] + _rms(mix, gpost_ref[...])
    hn = _rms(x1, gpre_ref[...]).astype(BF16)

    for c in range(n_chunks):
        cols = slice(c * width, (c + 1) * width)
        u = jnp.dot(hn, wup_ref[:, cols], preferred_element_type=F32)
        u_s[0:SUBLANES, :] = halo_s[:, cols]
        u_s[SUBLANES:SUBLANES + tm, :] = u
        halo_s[:, cols] = u[tm - SUBLANES:tm]
        y = (cb_ref[:, cols] + cw_ref[2:3, cols] * u
             + cw_ref[1:2, cols] * u_s[SUBLANES - 1:SUBLANES - 1 + tm, :]
             + cw_ref[0:1, cols] * u_s[SUBLANES - 2:SUBLANES - 2 + tm, :])
        gate = y[:, :FF_CHUNK]
        cdf = 0.5 * (1.0 + jnp.tanh(np.sqrt(2.0 / np.pi).astype(np.float32) * (gate + 0.044715 * (gate * gate * gate))))
        act_s[:, c * FF_CHUNK:(c + 1) * FF_CHUNK] = (gate * cdf * y[:, FF_CHUNK:]).astype(BF16)

    ffn = jnp.dot(act_s[...], wdown_ref[...], preferred_element_type=F32)
    o_ref[...] = x1 + _rms(ffn, gffn_ref[...])


def _outproj_ffn(x, attn, ret, g_post, g_pre, g_ffn, w_out_bf, w_up_bf, conv_w, conv_b, w_down_bf):
    b, s, d = x.shape
    tm = TOKEN_TILE
    d_ff = w_down_bf.shape[0]
    const = lambda shape: pl.BlockSpec(shape, lambda bi, i: (0,) * len(shape), pipeline_mode=pl.Buffered(1))
    return pl.pallas_call(
        _outproj_ffn_kernel,
        grid=(b, s // tm),
        in_specs=[
            pl.BlockSpec((None, tm, d), lambda bi, i: (bi, i, 0)),
            pl.BlockSpec((None, N_PAIRS, tm // MAX_DIL, MAX_DIL * LANES), lambda bi, i: (bi, 0, i, 0)),
            pl.BlockSpec((None, tm, RET_V_WIDTH), lambda bi, i: (bi, i, 0)),
            const((1, d)), const((1, d)), const((1, d)),
            const(w_out_bf.shape), const(w_up_bf.shape), const(conv_w.shape), const(conv_b.shape),
            const(w_down_bf.shape),
        ],
        out_specs=pl.BlockSpec((None, tm, d), lambda bi, i: (bi, i, 0)),
        out_shape=jax.ShapeDtypeStruct((b, s, d), F32),
        scratch_shapes=[pltpu.VMEM((tm, d_ff), BF16),
                        pltpu.VMEM((tm + SUBLANES, 2 * FF_CHUNK), F32),
                        pltpu.VMEM((SUBLANES, 2 * d_ff), F32),
                        pltpu.VMEM((N_PAIRS, tm, LANES), F32)],
        compiler_params=pltpu.CompilerParams(dimension_semantics=("arbitrary", "arbitrary"),
                                             vmem_limit_bytes=VMEM_LIMIT_BYTES),
        name="outproj_ffn",
    )(x, attn, ret, g_post, g_pre, g_ffn, w_out_bf, w_up_bf, conv_w, conv_b, w_down_bf)


def _interleave_gate_up(t, d_ff):
    lead = t.shape[:-1]
    g = t[..., :d_ff].reshape(*lead, d_ff // FF_CHUNK, FF_CHUNK)
    u = t[..., d_ff:].reshape(*lead, d_ff // FF_CHUNK, FF_CHUNK)
    return jnp.concatenate([g, u], axis=-1).reshape(*lead, 2 * d_ff)


def _layer(x, g_mix_pre, g_mix_post, w_in, w_out, g_ffn_pre, g_ffn_post, w_up, conv_w, conv_b, w_down):
    d = x.shape[-1]
    d_ff = w_down.shape[0]
    col_scale = np.ones((w_in.shape[1],), np.float32)
    col_scale[:ATTN_WIDTH] = ATTN_HEAD_DIM ** -0.5 * np.log2(np.e)
    kr0 = 3 * ATTN_WIDTH + RET_QK_WIDTH
    col_scale[kr0:kr0 + RET_QK_WIDTH] = RET_QK_DIM ** -0.5
    w_in_bf = (w_in * col_scale).astype(BF16)
    qa, ka, va, ret = _inproj_retention(x, g_mix_pre.reshape(1, d), w_in_bf)
    attn = _dilated_attention(qa, ka, va)
    return _outproj_ffn(x, attn, ret, g_mix_post.reshape(1, d), g_ffn_pre.reshape(1, d), g_ffn_post.reshape(1, d),
                        w_out.astype(BF16), _interleave_gate_up(w_up, d_ff).astype(BF16),
                        _interleave_gate_up(conv_w, d_ff), _interleave_gate_up(conv_b.reshape(1, -1), d_ff),
                        w_down.astype(BF16))


def kernel(x, mix_pre_gain, mix_post_gain, w_in, w_out, ffn_pre_gain, ffn_post_gain, w_up, conv_w, conv_b, w_down):
    for layer in range(mix_pre_gain.shape[0]):
        x = _layer(x, mix_pre_gain[layer], mix_post_gain[layer], w_in[layer], w_out[layer], ffn_pre_gain[layer],
                   ffn_post_gain[layer], w_up[layer], conv_w[layer], conv_b[layer], w_down[layer])
    return x
```

```python
import numpy as np
import jax
import jax.numpy as jnp
from jax import lax
from jax.experimental import pallas as pl
from jax.experimental.pallas import tpu as pltpu

F32 = jnp.float32
BF16 = jnp.bfloat16

ATTN_HEAD_DIM = 64
N_ATTN_HEADS = 8
ATTN_WIDTH = N_ATTN_HEADS * ATTN_HEAD_DIM
ATTN_STEPS = 128
DILATIONS = (1, 4, 16)
RET_V_DIM = 128
N_RET_HEADS = 4
RET_QK_DIM = 64
RET_QK_WIDTH = N_RET_HEADS * RET_QK_DIM
RET_V_WIDTH = N_RET_HEADS * RET_V_DIM
ROPE_BASE = 10000.0
CONV_WIDTH = 3
NORM_EPS = 1e-6
MASK_VALUE = -1e30

LANES = 128
SUBLANES = 8
HEAD_PAIR = LANES // ATTN_HEAD_DIM
N_PAIRS = N_ATTN_HEADS // HEAD_PAIR
MAX_DIL = max(DILATIONS)
ATTN_ROWS = 128
TOKEN_TILE = 512
RET_CHUNK = 256
FF_CHUNK = 256
ATTN_UNROLL = 16
GELU_C1 = float(np.sqrt(2.0 / np.pi).astype(np.float32))
GELU_C3 = float(np.float32(GELU_C1 * 0.044715))
VMEM_LIMIT_BYTES = 56 * 1024 * 1024


def _rms(x, gain):
    return x * lax.rsqrt(jnp.mean(x * x, axis=-1, keepdims=True) + NORM_EPS) * gain


def _inproj_retention_kernel(x_ref, gain_ref, w_ref, cos_ref, sin_ref, dmat_ref, qdec_ref, kdec_ref, sdec_ref,
                             qa_ref, ka_ref, va_ref, ret_ref, state_ref, stage_ref):
    @pl.when(pl.program_id(1) == 0)
    def _():
        state_ref[...] = jnp.zeros_like(state_ref)

    base = 3 * ATTN_WIDTH
    vbase = base + 2 * RET_QK_WIDTH
    gbase = vbase + RET_V_WIDTH
    lane = lax.broadcasted_iota(jnp.int32, (1, RET_QK_WIDTH), 1)
    first_half = (lane % RET_QK_DIM) < (RET_QK_DIM // 2)
    lane128 = lax.broadcasted_iota(jnp.int32, (1, LANES), 1)
    low_lanes = lane128 < RET_QK_DIM
    row128 = lax.broadcasted_iota(jnp.int32, (LANES, 1), 0)
    low_rows = row128 < RET_QK_DIM
    nt = (((1,), (1,)), ((), ()))
    tn = (((0,), (0,)), ((), ()))
    view_rows = RET_CHUNK // MAX_DIL

    for c in range(x_ref.shape[0] // RET_CHUNK):
        rows = slice(c * RET_CHUNK, (c + 1) * RET_CHUNK)
        h = _rms(x_ref[rows, :], gain_ref[...]).astype(BF16)
        proj = jnp.dot(h, w_ref[...], preferred_element_type=F32)

        for t, out_ref in enumerate((qa_ref, ka_ref, va_ref)):
            for hp in range(N_PAIRS):
                col = t * N_PAIRS + hp
                stage_ref[col, rows, :] = proj[:, col * LANES:(col + 1) * LANES]
                for r in range(MAX_DIL):
                    piece = stage_ref[col, pl.ds(c * RET_CHUNK + r, view_rows, stride=MAX_DIL), :]
                    out_ref[hp, c * view_rows:(c + 1) * view_rows, r * LANES:(r + 1) * LANES] = piece.astype(BF16)

        def rotary(t):
            swapped = jnp.where(first_half,
                                pltpu.roll(t, RET_QK_WIDTH - RET_QK_DIM // 2, axis=1),
                                pltpu.roll(t, RET_QK_DIM // 2, axis=1))
            return t * cos_ref[rows, :] + swapped * sin_ref[rows, :]

        qr = rotary(proj[:, base:base + RET_QK_WIDTH])
        kr = rotary(proj[:, base + RET_QK_WIDTH:base + 2 * RET_QK_WIDTH])

        for p in range(N_RET_HEADS // 2):
            cols = slice(p * LANES, (p + 1) * LANES)
            q2 = qr[:, cols]
            k2 = kr[:, cols]
            q_heads = (jnp.where(low_lanes, q2, 0.0), jnp.where(low_lanes, 0.0, q2))
            lhs = jnp.concatenate(q_heads, axis=0).astype(BF16)
            scores = lax.dot_general(lhs, k2.astype(BF16), nt, preferred_element_type=F32)
            kdec = (k2 * kdec_ref[:, cols]).astype(BF16)
            state = state_ref[p]
            state_bf = state.astype(BF16)
            kv = []
            for hh in range(2):
                head = 2 * p + hh
                v_h = proj[:, vbase + head * RET_V_DIM:vbase + (head + 1) * RET_V_DIM].astype(BF16)
                s_h = (scores[hh * RET_CHUNK:(hh + 1) * RET_CHUNK] * dmat_ref[head]).astype(BF16)
                inner = jnp.dot(s_h, v_h, preferred_element_type=F32)
                q_dec = (q_heads[hh] * qdec_ref[:, cols]).astype(BF16)
                cross = jnp.dot(q_dec, state_bf, preferred_element_type=F32)
                o = inner + cross
                o = o * lax.rsqrt(jnp.mean(o * o, axis=-1, keepdims=True) + NORM_EPS)
                g = proj[:, gbase + head * RET_V_DIM:gbase + (head + 1) * RET_V_DIM]
                o = o * (g * (1.0 / (1.0 + jnp.exp(-g))))
                ret_ref[rows, head * RET_V_DIM:(head + 1) * RET_V_DIM] = o.astype(BF16)
                kv.append(lax.dot_general(kdec, v_h, tn, preferred_element_type=F32))
            state_ref[p] = sdec_ref[p] * state + jnp.where(low_rows, kv[0], kv[1])


def _retention_constants(seq):
    freqs = ROPE_BASE ** (-np.arange(0, RET_QK_DIM, 2, dtype=np.float64) / RET_QK_DIM)
    ang = np.arange(seq, dtype=np.float64)[:, None] * freqs[None]
    cos_h = np.concatenate([np.cos(ang), np.cos(ang)], axis=1)
    sin_h = np.concatenate([-np.sin(ang), np.sin(ang)], axis=1)
    cos_t = np.tile(cos_h, (1, N_RET_HEADS)).astype(np.float32)
    sin_t = np.tile(sin_h, (1, N_RET_HEADS)).astype(np.float32)
    log_gamma = np.log1p(-np.exp2(-5.0 - np.arange(N_RET_HEADS, dtype=np.float64)))
    idx = np.arange(RET_CHUNK, dtype=np.float64)
    rel = idx[:, None] - idx[None, :]
    dmat = np.where(rel >= 0, np.exp(log_gamma[:, None, None] * np.maximum(rel, 0.0)), 0.0)
    qdec = np.exp(log_gamma[None, :] * (idx + 1.0)[:, None])
    kdec = np.exp(log_gamma[None, :] * (RET_CHUNK - 1 - idx)[:, None])
    qdec = np.repeat(qdec, RET_QK_DIM, axis=1)
    kdec = np.repeat(kdec, RET_QK_DIM, axis=1)
    sdec = np.repeat(np.exp(log_gamma * RET_CHUNK), RET_QK_DIM).reshape(N_RET_HEADS // 2, LANES, 1)
    sdec = np.broadcast_to(sdec, (N_RET_HEADS // 2, LANES, LANES))
    f = lambda a: jnp.asarray(np.ascontiguousarray(a), dtype=F32)
    return f(cos_t), f(sin_t), f(dmat), f(qdec), f(kdec), f(sdec)


def _inproj_retention(x, gain, w_in_bf):
    b, s, d = x.shape
    tm = TOKEN_TILE
    in_width = w_in_bf.shape[1]
    cos_t, sin_t, dmat, qdec, kdec, sdec = _retention_constants(s)
    const = lambda shape: pl.BlockSpec(shape, lambda bi, i: (0,) * len(shape))
    pair_spec = pl.BlockSpec((None, N_PAIRS, tm // MAX_DIL, MAX_DIL * LANES), lambda bi, i: (bi, 0, i, 0))
    pair_shape = jax.ShapeDtypeStruct((b, N_PAIRS, s // MAX_DIL, MAX_DIL * LANES), BF16)
    return pl.pallas_call(
        _inproj_retention_kernel,
        grid=(b, s // tm),
        in_specs=[
            pl.BlockSpec((None, tm, d), lambda bi, i: (bi, i, 0)),
            const((1, d)),
            const((d, in_width)),
            pl.BlockSpec((tm, RET_QK_WIDTH), lambda bi, i: (i, 0)),
            pl.BlockSpec((tm, RET_QK_WIDTH), lambda bi, i: (i, 0)),
            const((N_RET_HEADS, RET_CHUNK, RET_CHUNK)),
            const((RET_CHUNK, RET_QK_WIDTH)),
            const((RET_CHUNK, RET_QK_WIDTH)),
            const((N_RET_HEADS // 2, LANES, LANES)),
        ],
        out_specs=[pair_spec, pair_spec, pair_spec,
                   pl.BlockSpec((None, tm, RET_V_WIDTH), lambda bi, i: (bi, i, 0))],
        out_shape=[pair_shape, pair_shape, pair_shape, jax.ShapeDtypeStruct((b, s, RET_V_WIDTH), BF16)],
        scratch_shapes=[pltpu.VMEM((N_RET_HEADS // 2, LANES, LANES), F32),
                        pltpu.VMEM((3 * N_PAIRS, tm, LANES), F32)],
        compiler_params=pltpu.CompilerParams(dimension_semantics=("arbitrary", "arbitrary"),
                                             vmem_limit_bytes=VMEM_LIMIT_BYTES),
        name="inproj_retention",
    )(x, gain, w_in_bf, cos_t, sin_t, dmat, qdec, kdec, sdec)


def _attention_biases():
    q = np.arange(ATTN_ROWS)[None, :]
    k = np.arange(2 * ATTN_ROWS)[:, None]
    out = []
    out += [(ATTN_ROWS + q - k, k >= ATTN_ROWS)]
    out += [(4 * (32 + q % 32 - k % 64) + q // 32 - k // 64, k % 64 >= 32)]
    out += [(16 * (8 + q % 8 - k % 16) + q // 8 - k // 16, k % 16 >= 8)]
    biases = []
    for dist, in_seq in out:
        band = (dist >= 0) & (dist <= ATTN_STEPS)
        biases.append(np.where(band, 0.0, MASK_VALUE))
        biases.append(np.where(band & in_seq, 0.0, MASK_VALUE))
    return jnp.asarray(np.stack(biases), dtype=BF16)


def _attention_kernel(bias_ref, q_ref, k_ref, v_ref, o_ref, q0_s, q1_s, k_s, v_s, num_s, den_s, max_s):
    first_tile = pl.program_id(2) == 0
    rows = ATTN_ROWS

    @pl.when(first_tile)
    def _():
        k_s[:, 0:rows, :] = jnp.zeros((MAX_DIL, rows, LANES), F32)
        v_s[:, 0:rows, :] = jnp.zeros((MAX_DIL, rows, LANES), F32)

    lane = lax.broadcasted_iota(jnp.int32, (1, LANES), 1)
    low = lane < ATTN_HEAD_DIM
    for r in range(MAX_DIL):
        cols = slice(r * LANES, (r + 1) * LANES)
        qf = q_ref[:, cols].astype(F32)
        q0_s[r] = jnp.where(low, qf, 0.0)
        q1_s[r] = jnp.where(low, 0.0, qf)
        k_s[r, rows:2 * rows, :] = k_ref[:, cols].astype(F32)
        v_s[r, rows:2 * rows, :] = v_ref[:, cols].astype(F32)

    ones = jnp.ones((2 * rows, LANES), BF16)
    pick = (lax.broadcasted_iota(jnp.int32, (2 * rows, LANES), 0) % rows
            == lax.broadcasted_iota(jnp.int32, (2 * rows, LANES), 1))
    pick = jnp.where(pick, 1.0, 0.0).astype(BF16)
    nt = (((1,), (1,)), ((), ()))

    def attend(q0_pieces, q1_pieces, k_pieces, v_pieces, bias):
        q2 = jnp.concatenate(q0_pieces + q1_pieces, axis=0).astype(BF16)
        k2 = jnp.concatenate(k_pieces, axis=0).astype(BF16)
        v2 = jnp.concatenate(v_pieces, axis=0).astype(BF16)
        s = lax.dot_general(jnp.concatenate([q2, pick], axis=1), jnp.concatenate([k2, bias], axis=1), nt,
                            preferred_element_type=F32)
        m = jnp.max(s, axis=-1, keepdims=True)
        p = jnp.exp2(s - m).astype(BF16)
        pv = jnp.dot(p, jnp.concatenate([v2, ones], axis=1), preferred_element_type=F32)
        return (jnp.where(low, pv[:rows, :LANES], pv[rows:, :LANES]),
                jnp.where(low, pv[:rows, LANES:], pv[rows:, LANES:]),
                jnp.where(low, m[:rows], m[rows:]))

    def variant(cond):
        return jnp.where(cond, 1, 0)

    def dil16(r, carry):
        num, den, mx = attend([q0_s[r]], [q1_s[r]], [k_s[r]], [v_s[r]], bias_ref[variant(first_tile)])
        num_s[0, r] = num
        den_s[0, r] = den
        max_s[0, r] = mx
        return carry

    lax.fori_loop(0, MAX_DIL, dil16, 0, unroll=ATTN_UNROLL)

    def dil4(idx, carry):
        c = idx // 4
        j = idx % 4
        qrow = pl.ds(pl.multiple_of(32 * j, 32), 32)
        krow = pl.ds(pl.multiple_of(96 + 32 * j, 32), 64)
        num, den, mx = attend([q0_s[4 * a + c, qrow, :] for a in range(4)],
                              [q1_s[4 * a + c, qrow, :] for a in range(4)],
                              [k_s[4 * a + c, krow, :] for a in range(4)],
                              [v_s[4 * a + c, krow, :] for a in range(4)],
                              bias_ref[2 + variant(first_tile & (j == 0))])
        for a in range(4):
            piece = slice(32 * a, 32 * (a + 1))
            num_s[1, 4 * a + c, qrow, :] = num[piece]
            den_s[1, 4 * a + c, qrow, :] = den[piece]
            max_s[1, 4 * a + c, qrow, :] = mx[piece]
        return carry

    lax.fori_loop(0, 16, dil4, 0, unroll=ATTN_UNROLL)

    def dil1(j, carry):
        qrow = pl.ds(pl.multiple_of(8 * j, 8), 8)
        krow = pl.ds(pl.multiple_of(120 + 8 * j, 8), 16)
        num, den, mx = attend([q0_s[r, qrow, :] for r in range(MAX_DIL)],
                              [q1_s[r, qrow, :] for r in range(MAX_DIL)],
                              [k_s[r, krow, :] for r in range(MAX_DIL)],
                              [v_s[r, krow, :] for r in range(MAX_DIL)],
                              bias_ref[4 + variant(first_tile & (j == 0))])
        for r in range(MAX_DIL):
            piece = slice(8 * r, 8 * (r + 1))
            num_s[2, r, qrow, :] = num[piece]
            den_s[2, r, qrow, :] = den[piece]
            max_s[2, r, qrow, :] = mx[piece]
        return carry

    lax.fori_loop(0, 16, dil1, 0, unroll=ATTN_UNROLL)

    for r in range(MAX_DIL):
        m0, m1, m2 = max_s[0, r], max_s[1, r], max_s[2, r]
        top = jnp.maximum(jnp.maximum(m0, m1), m2)
        e0, e1, e2 = jnp.exp2(m0 - top), jnp.exp2(m1 - top), jnp.exp2(m2 - top)
        num = e0 * num_s[0, r] + e1 * num_s[1, r] + e2 * num_s[2, r]
        den = e0 * den_s[0, r] + e1 * den_s[1, r] + e2 * den_s[2, r]
        o_ref[:, r * LANES:(r + 1) * LANES] = (num * (1.0 / den)).astype(BF16)
        k_s[r, 0:rows, :] = k_s[r, rows:2 * rows, :]
        v_s[r, 0:rows, :] = v_s[r, rows:2 * rows, :]


def _dilated_attention(qa, ka, va):
    b, n_pairs, view_len, width = qa.shape
    tile = pl.BlockSpec((None, None, ATTN_ROWS, width), lambda bi, hp, i: (bi, hp, i, 0))
    biases = _attention_biases()
    scratch = lambda n: pltpu.VMEM((MAX_DIL, n, LANES), F32)
    return pl.pallas_call(
        _attention_kernel,
        grid=(b, n_pairs, view_len // ATTN_ROWS),
        in_specs=[pl.BlockSpec(biases.shape, lambda bi, hp, i: (0, 0, 0)), tile, tile, tile],
        out_specs=tile,
        out_shape=jax.ShapeDtypeStruct(qa.shape, BF16),
        scratch_shapes=[scratch(ATTN_ROWS), scratch(ATTN_ROWS), scratch(2 * ATTN_ROWS), scratch(2 * ATTN_ROWS)]
        + [pltpu.VMEM((len(DILATIONS), MAX_DIL, ATTN_ROWS, LANES), F32)] * 3,
        compiler_params=pltpu.CompilerParams(dimension_semantics=("arbitrary", "arbitrary", "arbitrary"),
                                             vmem_limit_bytes=VMEM_LIMIT_BYTES),
        name="dilated_attention",
    )(biases, qa, ka, va)


def _outproj_ffn_kernel(x_ref, attn_ref, ret_ref, gpost_ref, gpre_ref, gffn_ref, wout_ref, wup_ref, cw_ref, cb_ref,
                        wdown_ref, o_ref, act_s, u_s, halo_s, stage_s):
    tm = x_ref.shape[0]
    n_chunks = wup_ref.shape[0]

    @pl.when(pl.program_id(1) == 0)
    def _():
        halo_s[...] = jnp.zeros_like(halo_s)

    for hp in range(N_PAIRS):
        for r in range(MAX_DIL):
            piece = attn_ref[hp, :, r * LANES:(r + 1) * LANES].astype(F32)
            stage_s[hp, pl.ds(r, tm // MAX_DIL, stride=MAX_DIL), :] = piece
    mixed = jnp.concatenate([stage_s[hp].astype(BF16) for hp in range(N_PAIRS)] + [ret_ref[...]],
                            axis=-1)
    mix = jnp.dot(mixed, wout_ref[...], preferred_element_type=F32)
    x1 = x_ref[...] + _rms(mix, gpost_ref[...])
    hn = _rms(x1, gpre_ref[...]).astype(BF16)

    for c in range(n_chunks):
        ub = u_s.at[c % 2]
        u = jnp.dot(hn, wup_ref[c], preferred_element_type=F32)
        ub[0:SUBLANES, :] = halo_s[c]
        ub[SUBLANES:SUBLANES + tm, :] = u
        halo_s[c] = u[tm - SUBLANES:tm]
        y = (cb_ref[c] + cw_ref[c, 2:3, :] * u
             + cw_ref[c, 1:2, :] * ub[SUBLANES - 1:SUBLANES - 1 + tm, :]
             + cw_ref[c, 0:1, :] * ub[SUBLANES - 2:SUBLANES - 2 + tm, :])
        gate = y[:, :FF_CHUNK]
        half_up = y[:, FF_CHUNK:]
        inner = gate * (GELU_C1 + GELU_C3 * (gate * gate))
        act_s[:, c * FF_CHUNK:(c + 1) * FF_CHUNK] = (gate * (1.0 + jnp.tanh(inner)) * half_up).astype(BF16)

    ffn = jnp.dot(act_s[...], wdown_ref[...], preferred_element_type=F32)
    o_ref[...] = x1 + _rms(ffn, gffn_ref[...])


def _outproj_ffn(x, attn, ret, g_post, g_pre, g_ffn, w_out_bf, w_up_bf, conv_w, conv_b, w_down_bf):
    b, s, d = x.shape
    tm = TOKEN_TILE
    n_chunks = w_up_bf.shape[0]
    const = lambda shape: pl.BlockSpec(shape, lambda bi, i: (0,) * len(shape), pipeline_mode=pl.Buffered(1))
    return pl.pallas_call(
        _outproj_ffn_kernel,
        grid=(b, s // tm),
        in_specs=[
            pl.BlockSpec((None, tm, d), lambda bi, i: (bi, i, 0)),
            pl.BlockSpec((None, N_PAIRS, tm // MAX_DIL, MAX_DIL * LANES), lambda bi, i: (bi, 0, i, 0)),
            pl.BlockSpec((None, tm, RET_V_WIDTH), lambda bi, i: (bi, i, 0)),
            const((1, d)), const((1, d)), const((1, d)),
            const(w_out_bf.shape), const(w_up_bf.shape), const(conv_w.shape), const(conv_b.shape),
            const(w_down_bf.shape),
        ],
        out_specs=pl.BlockSpec((None, tm, d), lambda bi, i: (bi, i, 0)),
        out_shape=jax.ShapeDtypeStruct((b, s, d), F32),
        scratch_shapes=[pltpu.VMEM((tm, n_chunks * FF_CHUNK), BF16),
                        pltpu.VMEM((2, tm + SUBLANES, 2 * FF_CHUNK), F32),
                        pltpu.VMEM((n_chunks, SUBLANES, 2 * FF_CHUNK), F32),
                        pltpu.VMEM((N_PAIRS, tm, LANES), F32)],
        compiler_params=pltpu.CompilerParams(dimension_semantics=("arbitrary", "arbitrary"),
                                             vmem_limit_bytes=VMEM_LIMIT_BYTES),
        name="outproj_ffn",
    )(x, attn, ret, g_post, g_pre, g_ffn, w_out_bf, w_up_bf, conv_w, conv_b, w_down_bf)


def _chunk_gate_up(t, d_ff, up_scale=1.0):
    rows = t.shape[0]
    g = t[:, :d_ff].reshape(rows, d_ff // FF_CHUNK, FF_CHUNK)
    u = (t[:, d_ff:] * up_scale).reshape(rows, d_ff // FF_CHUNK, FF_CHUNK)
    return jnp.concatenate([g, u], axis=-1).transpose(1, 0, 2)


def _layer(x, g_mix_pre, g_mix_post, w_in, w_out, g_ffn_pre, g_ffn_post, w_up, conv_w, conv_b, w_down):
    d = x.shape[-1]
    d_ff = w_down.shape[0]
    col_scale = np.ones((w_in.shape[1],), np.float32)
    col_scale[:ATTN_WIDTH] = ATTN_HEAD_DIM ** -0.5 * np.log2(np.e)
    kr0 = 3 * ATTN_WIDTH + RET_QK_WIDTH
    col_scale[kr0:kr0 + RET_QK_WIDTH] = RET_QK_DIM ** -0.5
    w_in_bf = (w_in * col_scale).astype(BF16)
    qa, ka, va, ret = _inproj_retention(x, g_mix_pre.reshape(1, d), w_in_bf)
    attn = _dilated_attention(qa, ka, va)
    return _outproj_ffn(x, attn, ret, g_mix_post.reshape(1, d), g_ffn_pre.reshape(1, d), g_ffn_post.reshape(1, d),
                        w_out.astype(BF16), _chunk_gate_up(w_up, d_ff).astype(BF16),
                        _chunk_gate_up(conv_w, d_ff, 0.5), _chunk_gate_up(conv_b.reshape(1, -1), d_ff, 0.5),
                        w_down.astype(BF16))


def kernel(x, mix_pre_gain, mix_post_gain, w_in, w_out, ffn_pre_gain, ffn_post_gain, w_up, conv_w, conv_b, w_down):
    for layer in range(mix_pre_gain.shape[0]):
        x = _layer(x, mix_pre_gain[layer], mix_post_gain[layer], w_in[layer], w_out[layer], ffn_pre_gain[layer],
                   ffn_post_gain[layer], w_up[layer], conv_w[layer], conv_b[layer], w_down[layer])
    return x
```

```python
import numpy as np
import jax
import jax.numpy as jnp
from jax import lax
from jax.experimental import pallas as pl
from jax.experimental.pallas import tpu as pltpu

F32 = jnp.float32
BF16 = jnp.bfloat16

ATTN_HEAD_DIM = 64
N_ATTN_HEADS = 8
ATTN_WIDTH = N_ATTN_HEADS * ATTN_HEAD_DIM
ATTN_STEPS = 128
DILATIONS = (1, 4, 16)
RET_V_DIM = 128
N_RET_HEADS = 4
RET_QK_DIM = 64
RET_QK_WIDTH = N_RET_HEADS * RET_QK_DIM
RET_V_WIDTH = N_RET_HEADS * RET_V_DIM
ROPE_BASE = 10000.0
CONV_WIDTH = 3
NORM_EPS = 1e-6
MASK_VALUE = -1e30

LANES = 128
SUBLANES = 8
HEAD_PAIR = LANES // ATTN_HEAD_DIM
N_PAIRS = N_ATTN_HEADS // HEAD_PAIR
MAX_DIL = max(DILATIONS)
ATTN_ROWS = 128
TOKEN_TILE = 512
INPROJ_TILE = 1024
RET_CHUNK = 256
FF_CHUNK = 256
ATTN_UNROLL = 16
GELU_C1 = float(np.sqrt(2.0 / np.pi).astype(np.float32))
GELU_C3 = float(np.float32(GELU_C1 * 0.044715))
VMEM_LIMIT_BYTES = 56 * 1024 * 1024


def _rms(x, gain):
    return x * lax.rsqrt(jnp.mean(x * x, axis=-1, keepdims=True) + NORM_EPS) * gain


def _inproj_retention_kernel(x_ref, gain_ref, w_ref, cos_ref, sin_ref, dmat_ref, qdec_ref, kdec_ref, sdec_ref,
                             qa_ref, ka_ref, va_ref, ret_ref, state_ref, stage_ref):
    @pl.when(pl.program_id(1) == 0)
    def _():
        state_ref[...] = jnp.zeros_like(state_ref)

    base = 3 * ATTN_WIDTH
    vbase = base + 2 * RET_QK_WIDTH
    gbase = vbase + RET_V_WIDTH
    lane = lax.broadcasted_iota(jnp.int32, (1, RET_QK_WIDTH), 1)
    first_half = (lane % RET_QK_DIM) < (RET_QK_DIM // 2)
    lane128 = lax.broadcasted_iota(jnp.int32, (1, LANES), 1)
    low_lanes = lane128 < RET_QK_DIM
    row128 = lax.broadcasted_iota(jnp.int32, (LANES, 1), 0)
    low_rows = row128 < RET_QK_DIM
    nt = (((1,), (1,)), ((), ()))
    tn = (((0,), (0,)), ((), ()))
    view_rows = RET_CHUNK // MAX_DIL

    for c in range(x_ref.shape[0] // RET_CHUNK):
        rows = slice(c * RET_CHUNK, (c + 1) * RET_CHUNK)
        h = _rms(x_ref[rows, :], gain_ref[...]).astype(BF16)
        proj = jnp.dot(h, w_ref[...], preferred_element_type=F32)

        for t, out_ref in enumerate((qa_ref, ka_ref, va_ref)):
            for hp in range(N_PAIRS):
                col = t * N_PAIRS + hp
                stage_ref[col, rows, :] = proj[:, col * LANES:(col + 1) * LANES]
                for r in range(MAX_DIL):
                    piece = stage_ref[col, pl.ds(c * RET_CHUNK + r, view_rows, stride=MAX_DIL), :]
                    out_ref[hp, c * view_rows:(c + 1) * view_rows, r * LANES:(r + 1) * LANES] = piece.astype(BF16)

        def rotary(t):
            swapped = jnp.where(first_half,
                                pltpu.roll(t, RET_QK_WIDTH - RET_QK_DIM // 2, axis=1),
                                pltpu.roll(t, RET_QK_DIM // 2, axis=1))
            return t * cos_ref[rows, :] + swapped * sin_ref[rows, :]

        qr = rotary(proj[:, base:base + RET_QK_WIDTH])
        kr = rotary(proj[:, base + RET_QK_WIDTH:base + 2 * RET_QK_WIDTH])

        for p in range(N_RET_HEADS // 2):
            cols = slice(p * LANES, (p + 1) * LANES)
            q2 = qr[:, cols]
            k2 = kr[:, cols]
            q_heads = (jnp.where(low_lanes, q2, 0.0), jnp.where(low_lanes, 0.0, q2))
            lhs = jnp.concatenate(q_heads, axis=0).astype(BF16)
            scores = lax.dot_general(lhs, k2.astype(BF16), nt, preferred_element_type=F32)
            kdec = (k2 * kdec_ref[:, cols]).astype(BF16)
            state = state_ref[p]
            state_bf = state.astype(BF16)
            kv = []
            for hh in range(2):
                head = 2 * p + hh
                v_h = proj[:, vbase + head * RET_V_DIM:vbase + (head + 1) * RET_V_DIM].astype(BF16)
                s_h = (scores[hh * RET_CHUNK:(hh + 1) * RET_CHUNK] * dmat_ref[head]).astype(BF16)
                inner = jnp.dot(s_h, v_h, preferred_element_type=F32)
                q_dec = (q_heads[hh] * qdec_ref[:, cols]).astype(BF16)
                cross = jnp.dot(q_dec, state_bf, preferred_element_type=F32)
                o = inner + cross
                o = o * lax.rsqrt(jnp.mean(o * o, axis=-1, keepdims=True) + NORM_EPS)
                g = proj[:, gbase + head * RET_V_DIM:gbase + (head + 1) * RET_V_DIM]
                o = o * (g * (1.0 / (1.0 + jnp.exp(-g))))
                ret_ref[rows, head * RET_V_DIM:(head + 1) * RET_V_DIM] = o.astype(BF16)
                kv.append(lax.dot_general(kdec, v_h, tn, preferred_element_type=F32))
            state_ref[p] = sdec_ref[p] * state + jnp.where(low_rows, kv[0], kv[1])


def _retention_constants(seq):
    freqs = ROPE_BASE ** (-np.arange(0, RET_QK_DIM, 2, dtype=np.float64) / RET_QK_DIM)
    ang = np.arange(seq, dtype=np.float64)[:, None] * freqs[None]
    cos_h = np.concatenate([np.cos(ang), np.cos(ang)], axis=1)
    sin_h = np.concatenate([-np.sin(ang), np.sin(ang)], axis=1)
    cos_t = np.tile(cos_h, (1, N_RET_HEADS)).astype(np.float32)
    sin_t = np.tile(sin_h, (1, N_RET_HEADS)).astype(np.float32)
    log_gamma = np.log1p(-np.exp2(-5.0 - np.arange(N_RET_HEADS, dtype=np.float64)))
    idx = np.arange(RET_CHUNK, dtype=np.float64)
    rel = idx[:, None] - idx[None, :]
    dmat = np.where(rel >= 0, np.exp(log_gamma[:, None, None] * np.maximum(rel, 0.0)), 0.0)
    qdec = np.exp(log_gamma[None, :] * (idx + 1.0)[:, None])
    kdec = np.exp(log_gamma[None, :] * (RET_CHUNK - 1 - idx)[:, None])
    qdec = np.repeat(qdec, RET_QK_DIM, axis=1)
    kdec = np.repeat(kdec, RET_QK_DIM, axis=1)
    sdec = np.repeat(np.exp(log_gamma * RET_CHUNK), RET_QK_DIM).reshape(N_RET_HEADS // 2, LANES, 1)
    sdec = np.broadcast_to(sdec, (N_RET_HEADS // 2, LANES, LANES))
    f = lambda a: jnp.asarray(np.ascontiguousarray(a), dtype=F32)
    return f(cos_t), f(sin_t), f(dmat), f(qdec), f(kdec), f(sdec)


def _inproj_retention(x, gain, w_in_bf):
    b, s, d = x.shape
    tm = INPROJ_TILE
    in_width = w_in_bf.shape[1]
    cos_t, sin_t, dmat, qdec, kdec, sdec = _retention_constants(s)
    const = lambda shape: pl.BlockSpec(shape, lambda bi, i: (0,) * len(shape))
    pair_spec = pl.BlockSpec((None, N_PAIRS, tm // MAX_DIL, MAX_DIL * LANES), lambda bi, i: (bi, 0, i, 0))
    pair_shape = jax.ShapeDtypeStruct((b, N_PAIRS, s // MAX_DIL, MAX_DIL * LANES), BF16)
    return pl.pallas_call(
        _inproj_retention_kernel,
        grid=(b, s // tm),
        in_specs=[
            pl.BlockSpec((None, tm, d), lambda bi, i: (bi, i, 0)),
            const((1, d)),
            const((d, in_width)),
            pl.BlockSpec((tm, RET_QK_WIDTH), lambda bi, i: (i, 0)),
            pl.BlockSpec((tm, RET_QK_WIDTH), lambda bi, i: (i, 0)),
            const((N_RET_HEADS, RET_CHUNK, RET_CHUNK)),
            const((RET_CHUNK, RET_QK_WIDTH)),
            const((RET_CHUNK, RET_QK_WIDTH)),
            const((N_RET_HEADS // 2, LANES, LANES)),
        ],
        out_specs=[pair_spec, pair_spec, pair_spec,
                   pl.BlockSpec((None, tm, RET_V_WIDTH), lambda bi, i: (bi, i, 0))],
        out_shape=[pair_shape, pair_shape, pair_shape, jax.ShapeDtypeStruct((b, s, RET_V_WIDTH), BF16)],
        scratch_shapes=[pltpu.VMEM((N_RET_HEADS // 2, LANES, LANES), F32),
                        pltpu.VMEM((3 * N_PAIRS, tm, LANES), F32)],
        compiler_params=pltpu.CompilerParams(dimension_semantics=("arbitrary", "arbitrary"),
                                             vmem_limit_bytes=VMEM_LIMIT_BYTES),
        name="inproj_retention",
    )(x, gain, w_in_bf, cos_t, sin_t, dmat, qdec, kdec, sdec)


def _attention_biases():
    q = np.arange(ATTN_ROWS)[None, :]
    k = np.arange(2 * ATTN_ROWS)[:, None]
    out = []
    out += [(ATTN_ROWS + q - k, k >= ATTN_ROWS)]
    out += [(4 * (32 + q % 32 - k % 64) + q // 32 - k // 64, k % 64 >= 32)]
    out += [(16 * (8 + q % 8 - k % 16) + q // 8 - k // 16, k % 16 >= 8)]
    biases = []
    for dist, in_seq in out:
        band = (dist >= 0) & (dist <= ATTN_STEPS)
        biases.append(np.where(band, 0.0, MASK_VALUE))
        biases.append(np.where(band & in_seq, 0.0, MASK_VALUE))
    return jnp.asarray(np.stack(biases), dtype=BF16)


def _attention_kernel(bias_ref, q_ref, k_ref, v_ref, o_ref, q0_s, q1_s, k_s, v_s, num_s, den_s, max_s):
    first_tile = pl.program_id(2) == 0
    rows = ATTN_ROWS

    @pl.when(first_tile)
    def _():
        k_s[:, 0:rows, :] = jnp.zeros((MAX_DIL, rows, LANES), F32)
        v_s[:, 0:rows, :] = jnp.zeros((MAX_DIL, rows, LANES), F32)

    lane = lax.broadcasted_iota(jnp.int32, (1, LANES), 1)
    low = lane < ATTN_HEAD_DIM
    for r in range(MAX_DIL):
        cols = slice(r * LANES, (r + 1) * LANES)
        qf = q_ref[:, cols].astype(F32)
        q0_s[r] = jnp.where(low, qf, 0.0)
        q1_s[r] = jnp.where(low, 0.0, qf)
        k_s[r, rows:2 * rows, :] = k_ref[:, cols].astype(F32)
        v_s[r, rows:2 * rows, :] = v_ref[:, cols].astype(F32)

    ones = jnp.ones((2 * rows, LANES), BF16)
    pick = (lax.broadcasted_iota(jnp.int32, (2 * rows, LANES), 0) % rows
            == lax.broadcasted_iota(jnp.int32, (2 * rows, LANES), 1))
    pick = jnp.where(pick, 1.0, 0.0).astype(BF16)
    nt = (((1,), (1,)), ((), ()))

    def attend(q0_pieces, q1_pieces, k_pieces, v_pieces, bias):
        q2 = jnp.concatenate(q0_pieces + q1_pieces, axis=0).astype(BF16)
        k2 = jnp.concatenate(k_pieces, axis=0).astype(BF16)
        v2 = jnp.concatenate(v_pieces, axis=0).astype(BF16)
        s = lax.dot_general(jnp.concatenate([q2, pick], axis=1), jnp.concatenate([k2, bias], axis=1), nt,
                            preferred_element_type=F32)
        m = jnp.max(s, axis=-1, keepdims=True)
        p = jnp.exp2(s - m).astype(BF16)
        pv = jnp.dot(p, jnp.concatenate([v2, ones], axis=1), preferred_element_type=F32)
        return (jnp.where(low, pv[:rows, :LANES], pv[rows:, :LANES]),
                jnp.where(low, pv[:rows, LANES:], pv[rows:, LANES:]),
                jnp.where(low, m[:rows], m[rows:]))

    def variant(cond):
        return jnp.where(cond, 1, 0)

    def dil16(r, carry):
        num, den, mx = attend([q0_s[r]], [q1_s[r]], [k_s[r]], [v_s[r]], bias_ref[variant(first_tile)])
        num_s[0, r] = num
        den_s[0, r] = den
        max_s[0, r] = mx
        return carry

    lax.fori_loop(0, MAX_DIL, dil16, 0, unroll=ATTN_UNROLL)

    def dil4(idx, carry):
        c = idx // 4
        j = idx % 4
        qrow = pl.ds(pl.multiple_of(32 * j, 32), 32)
        krow = pl.ds(pl.multiple_of(96 + 32 * j, 32), 64)
        num, den, mx = attend([q0_s[4 * a + c, qrow, :] for a in range(4)],
                              [q1_s[4 * a + c, qrow, :] for a in range(4)],
                              [k_s[4 * a + c, krow, :] for a in range(4)],
                              [v_s[4 * a + c, krow, :] for a in range(4)],
                              bias_ref[2 + variant(first_tile & (j == 0))])
        for a in range(4):
            piece = slice(32 * a, 32 * (a + 1))
            num_s[1, 4 * a + c, qrow, :] = num[piece]
            den_s[1, 4 * a + c, qrow, :] = den[piece]
            max_s[1, 4 * a + c, qrow, :] = mx[piece]
        return carry

    lax.fori_loop(0, 16, dil4, 0, unroll=ATTN_UNROLL)

    def dil1(j, carry):
        qrow = pl.ds(pl.multiple_of(8 * j, 8), 8)
        krow = pl.ds(pl.multiple_of(120 + 8 * j, 8), 16)
        num, den, mx = attend([q0_s[r, qrow, :] for r in range(MAX_DIL)],
                              [q1_s[r, qrow, :] for r in range(MAX_DIL)],
                              [k_s[r, krow, :] for r in range(MAX_DIL)],
                              [v_s[r, krow, :] for r in range(MAX_DIL)],
                              bias_ref[4 + variant(first_tile & (j == 0))])
        for r in range(MAX_DIL):
            piece = slice(8 * r, 8 * (r + 1))
            num_s[2, r, qrow, :] = num[piece]
            den_s[2, r, qrow, :] = den[piece]
            max_s[2, r, qrow, :] = mx[piece]
        return carry

    lax.fori_loop(0, 16, dil1, 0, unroll=ATTN_UNROLL)

    for r in range(MAX_DIL):
        m0, m1, m2 = max_s[0, r], max_s[1, r], max_s[2, r]
        top = jnp.maximum(jnp.maximum(m0, m1), m2)
        e0, e1, e2 = jnp.exp2(m0 - top), jnp.exp2(m1 - top), jnp.exp2(m2 - top)
        num = e0 * num_s[0, r] + e1 * num_s[1, r] + e2 * num_s[2, r]
        den = e0 * den_s[0, r] + e1 * den_s[1, r] + e2 * den_s[2, r]
        o_ref[:, r * LANES:(r + 1) * LANES] = (num * (1.0 / den)).astype(BF16)
        k_s[r, 0:rows, :] = k_s[r, rows:2 * rows, :]
        v_s[r, 0:rows, :] = v_s[r, rows:2 * rows, :]


def _dilated_attention(qa, ka, va):
    b, n_pairs, view_len, width = qa.shape
    tile = pl.BlockSpec((None, None, ATTN_ROWS, width), lambda bi, hp, i: (bi, hp, i, 0))
    biases = _attention_biases()
    scratch = lambda n: pltpu.VMEM((MAX_DIL, n, LANES), F32)
    return pl.pallas_call(
        _attention_kernel,
        grid=(b, n_pairs, view_len // ATTN_ROWS),
        in_specs=[pl.BlockSpec(biases.shape, lambda bi, hp, i: (0, 0, 0)), tile, tile, tile],
        out_specs=tile,
        out_shape=jax.ShapeDtypeStruct(qa.shape, BF16),
        scratch_shapes=[scratch(ATTN_ROWS), scratch(ATTN_ROWS), scratch(2 * ATTN_ROWS), scratch(2 * ATTN_ROWS)]
        + [pltpu.VMEM((len(DILATIONS), MAX_DIL, ATTN_ROWS, LANES), F32)] * 3,
        compiler_params=pltpu.CompilerParams(dimension_semantics=("arbitrary", "arbitrary", "arbitrary"),
                                             vmem_limit_bytes=VMEM_LIMIT_BYTES),
        name="dilated_attention",
    )(biases, qa, ka, va)


def _outproj_ffn_kernel(x_ref, attn_ref, ret_ref, gpost_ref, gpre_ref, gffn_ref, wout_ref, wup_ref, cw_ref, cb_ref,
                        wdown_ref, o_ref, act_s, u_s, halo_s, stage_s):
    tm = x_ref.shape[0]
    d_ff = wdown_ref.shape[0]

    @pl.when(pl.program_id(1) == 0)
    def _():
        halo_s[...] = jnp.zeros_like(halo_s)

    for hp in range(N_PAIRS):
        for r in range(MAX_DIL):
            piece = attn_ref[hp, :, r * LANES:(r + 1) * LANES].astype(F32)
            stage_s[hp, pl.ds(r, tm // MAX_DIL, stride=MAX_DIL), :] = piece
    mixed = jnp.concatenate([stage_s[hp].astype(BF16) for hp in range(N_PAIRS)] + [ret_ref[...]],
                            axis=-1)
    mix = jnp.dot(mixed, wout_ref[...], preferred_element_type=F32)
    x1 = x_ref[...] + _rms(mix, gpost_ref[...])
    hn = _rms(x1, gpre_ref[...]).astype(BF16)

    def conv(c, part, cols, u):
        ub = u_s.at[2 * (c % 2) + part]
        ub[0:SUBLANES, :] = halo_s[:, cols]
        ub[SUBLANES:SUBLANES + tm, :] = u
        halo_s[:, cols] = u[tm - SUBLANES:tm]
        return (cb_ref[:, cols] + cw_ref[2:3, cols] * u
                + cw_ref[1:2, cols] * ub[SUBLANES - 1:SUBLANES - 1 + tm, :]
                + cw_ref[0:1, cols] * ub[SUBLANES - 2:SUBLANES - 2 + tm, :])

    for c in range(d_ff // FF_CHUNK):
        gcols = slice(c * FF_CHUNK, (c + 1) * FF_CHUNK)
        ucols = slice(d_ff + c * FF_CHUNK, d_ff + (c + 1) * FF_CHUNK)
        gate = conv(c, 0, gcols, jnp.dot(hn, wup_ref[:, gcols], preferred_element_type=F32))
        half_up = conv(c, 1, ucols, jnp.dot(hn, wup_ref[:, ucols], preferred_element_type=F32))
        inner = gate * (GELU_C1 + GELU_C3 * (gate * gate))
        act_s[:, gcols] = (gate * (1.0 + jnp.tanh(inner)) * half_up).astype(BF16)

    ffn = jnp.dot(act_s[...], wdown_ref[...], preferred_element_type=F32)
    o_ref[...] = x1 + _rms(ffn, gffn_ref[...])


def _outproj_ffn(x, attn, ret, g_post, g_pre, g_ffn, w_out_bf, w_up_bf, conv_w, conv_b, w_down_bf):
    b, s, d = x.shape
    tm = TOKEN_TILE
    d_ff = w_down_bf.shape[0]
    const = lambda shape: pl.BlockSpec(shape, lambda bi, i: (0,) * len(shape), pipeline_mode=pl.Buffered(1))
    return pl.pallas_call(
        _outproj_ffn_kernel,
        grid=(b, s // tm),
        in_specs=[
            pl.BlockSpec((None, tm, d), lambda bi, i: (bi, i, 0)),
            pl.BlockSpec((None, N_PAIRS, tm // MAX_DIL, MAX_DIL * LANES), lambda bi, i: (bi, 0, i, 0)),
            pl.BlockSpec((None, tm, RET_V_WIDTH), lambda bi, i: (bi, i, 0)),
            const((1, d)), const((1, d)), const((1, d)),
            const(w_out_bf.shape), const(w_up_bf.shape), const(conv_w.shape), const(conv_b.shape),
            const(w_down_bf.shape),
        ],
        out_specs=pl.BlockSpec((None, tm, d), lambda bi, i: (bi, i, 0)),
        out_shape=jax.ShapeDtypeStruct((b, s, d), F32),
        scratch_shapes=[pltpu.VMEM((tm, d_ff), BF16),
                        pltpu.VMEM((4, tm + SUBLANES, FF_CHUNK), F32),
                        pltpu.VMEM((SUBLANES, 2 * d_ff), F32),
                        pltpu.VMEM((N_PAIRS, tm, LANES), F32)],
        compiler_params=pltpu.CompilerParams(dimension_semantics=("arbitrary", "arbitrary"),
                                             vmem_limit_bytes=VMEM_LIMIT_BYTES),
        name="outproj_ffn",
    )(x, attn, ret, g_post, g_pre, g_ffn, w_out_bf, w_up_bf, conv_w, conv_b, w_down_bf)


def _layer(x, g_mix_pre, g_mix_post, w_in, w_out, g_ffn_pre, g_ffn_post, w_up, conv_w, conv_b, w_down):
    d = x.shape[-1]
    d_ff = w_down.shape[0]
    col_scale = np.ones((w_in.shape[1],), np.float32)
    col_scale[:ATTN_WIDTH] = ATTN_HEAD_DIM ** -0.5 * np.log2(np.e)
    kr0 = 3 * ATTN_WIDTH + RET_QK_WIDTH
    col_scale[kr0:kr0 + RET_QK_WIDTH] = RET_QK_DIM ** -0.5
    w_in_bf = (w_in * col_scale).astype(BF16)
    qa, ka, va, ret = _inproj_retention(x, g_mix_pre.reshape(1, d), w_in_bf)
    attn = _dilated_attention(qa, ka, va)
    up_half = np.concatenate([np.ones((d_ff,), np.float32), np.full((d_ff,), 0.5, np.float32)])
    return _outproj_ffn(x, attn, ret, g_mix_post.reshape(1, d), g_ffn_pre.reshape(1, d), g_ffn_post.reshape(1, d),
                        w_out.astype(BF16), w_up.astype(BF16), conv_w * up_half, conv_b.reshape(1, -1) * up_half,
                        w_down.astype(BF16))


def kernel(x, mix_pre_gain, mix_post_gain, w_in, w_out, ffn_pre_gain, ffn_post_gain, w_up, conv_w, conv_b, w_down):
    for layer in range(mix_pre_gain.shape[0]):
        x = _layer(x, mix_pre_gain[layer], mix_post_gain[layer], w_in[layer], w_out[layer], ffn_pre_gain[layer],
                   ffn_post_gain[layer], w_up[layer], conv_w[layer], conv_b[layer], w_down[layer])
    return x
```

```python
import numpy as np
import jax
import jax.numpy as jnp
from jax import lax
from jax.experimental import pallas as pl
from jax.experimental.pallas import tpu as pltpu

F32 = jnp.float32
BF16 = jnp.bfloat16

ATTN_HEAD_DIM = 64
N_ATTN_HEADS = 8
ATTN_WIDTH = N_ATTN_HEADS * ATTN_HEAD_DIM
ATTN_STEPS = 128
DILATIONS = (1, 4, 16)
RET_V_DIM = 128
N_RET_HEADS = 4
RET_QK_DIM = 64
RET_QK_WIDTH = N_RET_HEADS * RET_QK_DIM
RET_V_WIDTH = N_RET_HEADS * RET_V_DIM
ROPE_BASE = 10000.0
CONV_WIDTH = 3
NORM_EPS = 1e-6
MASK_VALUE = -1e30

LANES = 128
SUBLANES = 8
HEAD_PAIR = LANES // ATTN_HEAD_DIM
N_PAIRS = N_ATTN_HEADS // HEAD_PAIR
MAX_DIL = max(DILATIONS)
ATTN_ROWS = 128
TOKEN_TILE = 512
INPROJ_TILE = 1024
RET_CHUNK = 256
FF_CHUNK = 256
GELU_C1 = float(np.sqrt(2.0 / np.pi).astype(np.float32))
GELU_C3 = float(np.float32(GELU_C1 * 0.044715))
VMEM_LIMIT_BYTES = 56 * 1024 * 1024


def _rms(x, gain):
    return x * lax.rsqrt(jnp.mean(x * x, axis=-1, keepdims=True) + NORM_EPS) * gain


def _inproj_retention_kernel(x_ref, gain_ref, w_ref, cos_ref, sin_ref, dmat_ref, qdec_ref, kdec_ref, sdec_ref,
                             qa_ref, ka_ref, va_ref, ret_ref, state_ref, stage_ref):
    @pl.when(pl.program_id(1) == 0)
    def _():
        state_ref[...] = jnp.zeros_like(state_ref)

    base = 3 * ATTN_WIDTH
    vbase = base + 2 * RET_QK_WIDTH
    gbase = vbase + RET_V_WIDTH
    lane = lax.broadcasted_iota(jnp.int32, (1, RET_QK_WIDTH), 1)
    first_half = (lane % RET_QK_DIM) < (RET_QK_DIM // 2)
    lane128 = lax.broadcasted_iota(jnp.int32, (1, LANES), 1)
    low_lanes = lane128 < RET_QK_DIM
    row128 = lax.broadcasted_iota(jnp.int32, (LANES, 1), 0)
    low_rows = row128 < RET_QK_DIM
    nt = (((1,), (1,)), ((), ()))
    tn = (((0,), (0,)), ((), ()))
    view_rows = RET_CHUNK // MAX_DIL
    pitch = stage_ref.shape[1] // MAX_DIL

    for c in range(x_ref.shape[0] // RET_CHUNK):
        rows = slice(c * RET_CHUNK, (c + 1) * RET_CHUNK)
        h = _rms(x_ref[rows, :], gain_ref[...]).astype(BF16)
        proj = jnp.dot(h, w_ref[...], preferred_element_type=F32)

        for t, out_ref in enumerate((qa_ref, ka_ref, va_ref)):
            for hp in range(N_PAIRS):
                col = t * N_PAIRS + hp
                block = proj[:, col * LANES:(col + 1) * LANES]
                for g in range(RET_CHUNK // SUBLANES):
                    m = c * view_rows + g * SUBLANES // MAX_DIL
                    r0 = g * SUBLANES % MAX_DIL
                    stage_ref[col, pl.ds(r0 * pitch + m, SUBLANES, stride=pitch), :] = (
                        block[g * SUBLANES:(g + 1) * SUBLANES])
                for r in range(MAX_DIL):
                    piece = stage_ref[col, r * pitch + c * view_rows:r * pitch + (c + 1) * view_rows, :]
                    out_ref[hp, c * view_rows:(c + 1) * view_rows, r * LANES:(r + 1) * LANES] = piece.astype(BF16)

        def rotary(t):
            swapped = jnp.where(first_half,
                                pltpu.roll(t, RET_QK_WIDTH - RET_QK_DIM // 2, axis=1),
                                pltpu.roll(t, RET_QK_DIM // 2, axis=1))
            return t * cos_ref[rows, :] + swapped * sin_ref[rows, :]

        qr = rotary(proj[:, base:base + RET_QK_WIDTH])
        kr = rotary(proj[:, base + RET_QK_WIDTH:base + 2 * RET_QK_WIDTH])

        for p in range(N_RET_HEADS // 2):
            cols = slice(p * LANES, (p + 1) * LANES)
            q2 = qr[:, cols]
            k2 = kr[:, cols]
            q_heads = (jnp.where(low_lanes, q2, 0.0), jnp.where(low_lanes, 0.0, q2))
            lhs = jnp.concatenate(q_heads, axis=0).astype(BF16)
            scores = lax.dot_general(lhs, k2.astype(BF16), nt, preferred_element_type=F32)
            kdec = (k2 * kdec_ref[:, cols]).astype(BF16)
            state = state_ref[p]
            state_bf = state.astype(BF16)
            kv = []
            for hh in range(2):
                head = 2 * p + hh
                v_h = proj[:, vbase + head * RET_V_DIM:vbase + (head + 1) * RET_V_DIM].astype(BF16)
                s_h = (scores[hh * RET_CHUNK:(hh + 1) * RET_CHUNK] * dmat_ref[head]).astype(BF16)
                inner = jnp.dot(s_h, v_h, preferred_element_type=F32)
                q_dec = (q_heads[hh] * qdec_ref[:, cols]).astype(BF16)
                cross = jnp.dot(q_dec, state_bf, preferred_element_type=F32)
                o = inner + cross
                o = o * lax.rsqrt(jnp.mean(o * o, axis=-1, keepdims=True) + NORM_EPS)
                g = proj[:, gbase + head * RET_V_DIM:gbase + (head + 1) * RET_V_DIM]
                o = o * (g * (1.0 / (1.0 + jnp.exp(-g))))
                ret_ref[rows, head * RET_V_DIM:(head + 1) * RET_V_DIM] = o.astype(BF16)
                kv.append(lax.dot_general(kdec, v_h, tn, preferred_element_type=F32))
            state_ref[p] = sdec_ref[p] * state + jnp.where(low_rows, kv[0], kv[1])


def _retention_constants(seq):
    freqs = ROPE_BASE ** (-np.arange(0, RET_QK_DIM, 2, dtype=np.float64) / RET_QK_DIM)
    ang = np.arange(seq, dtype=np.float64)[:, None] * freqs[None]
    cos_h = np.concatenate([np.cos(ang), np.cos(ang)], axis=1)
    sin_h = np.concatenate([-np.sin(ang), np.sin(ang)], axis=1)
    cos_t = np.tile(cos_h, (1, N_RET_HEADS)).astype(np.float32)
    sin_t = np.tile(sin_h, (1, N_RET_HEADS)).astype(np.float32)
    log_gamma = np.log1p(-np.exp2(-5.0 - np.arange(N_RET_HEADS, dtype=np.float64)))
    idx = np.arange(RET_CHUNK, dtype=np.float64)
    rel = idx[:, None] - idx[None, :]
    dmat = np.where(rel >= 0, np.exp(log_gamma[:, None, None] * np.maximum(rel, 0.0)), 0.0)
    qdec = np.exp(log_gamma[None, :] * (idx + 1.0)[:, None])
    kdec = np.exp(log_gamma[None, :] * (RET_CHUNK - 1 - idx)[:, None])
    qdec = np.repeat(qdec, RET_QK_DIM, axis=1)
    kdec = np.repeat(kdec, RET_QK_DIM, axis=1)
    sdec = np.repeat(np.exp(log_gamma * RET_CHUNK), RET_QK_DIM).reshape(N_RET_HEADS // 2, LANES, 1)
    sdec = np.broadcast_to(sdec, (N_RET_HEADS // 2, LANES, LANES))
    f = lambda a: jnp.asarray(np.ascontiguousarray(a), dtype=F32)
    return f(cos_t), f(sin_t), f(dmat), f(qdec), f(kdec), f(sdec)


def _inproj_retention(x, gain, w_in_bf):
    b, s, d = x.shape
    tm = INPROJ_TILE
    in_width = w_in_bf.shape[1]
    cos_t, sin_t, dmat, qdec, kdec, sdec = _retention_constants(s)
    const = lambda shape: pl.BlockSpec(shape, lambda bi, i: (0,) * len(shape))
    pair_spec = pl.BlockSpec((None, N_PAIRS, tm // MAX_DIL, MAX_DIL * LANES), lambda bi, i: (bi, 0, i, 0))
    pair_shape = jax.ShapeDtypeStruct((b, N_PAIRS, s // MAX_DIL, MAX_DIL * LANES), BF16)
    return pl.pallas_call(
        _inproj_retention_kernel,
        grid=(b, s // tm),
        in_specs=[
            pl.BlockSpec((None, tm, d), lambda bi, i: (bi, i, 0)),
            const((1, d)),
            const((d, in_width)),
            pl.BlockSpec((tm, RET_QK_WIDTH), lambda bi, i: (i, 0)),
            pl.BlockSpec((tm, RET_QK_WIDTH), lambda bi, i: (i, 0)),
            const((N_RET_HEADS, RET_CHUNK, RET_CHUNK)),
            const((RET_CHUNK, RET_QK_WIDTH)),
            const((RET_CHUNK, RET_QK_WIDTH)),
            const((N_RET_HEADS // 2, LANES, LANES)),
        ],
        out_specs=[pair_spec, pair_spec, pair_spec,
                   pl.BlockSpec((None, tm, RET_V_WIDTH), lambda bi, i: (bi, i, 0))],
        out_shape=[pair_shape, pair_shape, pair_shape, jax.ShapeDtypeStruct((b, s, RET_V_WIDTH), BF16)],
        scratch_shapes=[pltpu.VMEM((N_RET_HEADS // 2, LANES, LANES), F32),
                        pltpu.VMEM((3 * N_PAIRS, MAX_DIL * (tm // MAX_DIL + SUBLANES), LANES), F32)],
        compiler_params=pltpu.CompilerParams(dimension_semantics=("arbitrary", "arbitrary"),
                                             vmem_limit_bytes=VMEM_LIMIT_BYTES),
        name="inproj_retention",
    )(x, gain, w_in_bf, cos_t, sin_t, dmat, qdec, kdec, sdec)


def _attention_biases():
    q = np.arange(ATTN_ROWS)[None, :]
    k = np.arange(2 * ATTN_ROWS)[:, None]
    out = []
    out += [(ATTN_ROWS + q - k, k >= ATTN_ROWS)]
    out += [(4 * (32 + q % 32 - k % 64) + q // 32 - k // 64, k % 64 >= 32)]
    out += [(16 * (8 + q % 8 - k % 16) + q // 8 - k // 16, k % 16 >= 8)]
    biases = []
    for dist, in_seq in out:
        band = (dist >= 0) & (dist <= ATTN_STEPS)
        biases.append(np.where(band, 0.0, MASK_VALUE))
        biases.append(np.where(band & in_seq, 0.0, MASK_VALUE))
    return jnp.asarray(np.stack(biases), dtype=BF16)


def _attention_kernel(bias_ref, q_ref, k_ref, v_ref, o_ref, q0_s, q1_s, k_s, v_s, num_s, den_s, max_s):
    first_tile = pl.program_id(2) == 0
    rows = ATTN_ROWS

    @pl.when(first_tile)
    def _():
        k_s[:, 0:rows, :] = jnp.zeros((MAX_DIL, rows, LANES), F32)
        v_s[:, 0:rows, :] = jnp.zeros((MAX_DIL, rows, LANES), F32)

    lane = lax.broadcasted_iota(jnp.int32, (1, LANES), 1)
    low = lane < ATTN_HEAD_DIM
    for r in range(MAX_DIL):
        cols = slice(r * LANES, (r + 1) * LANES)
        qf = q_ref[:, cols].astype(F32)
        q0_s[r] = jnp.where(low, qf, 0.0)
        q1_s[r] = jnp.where(low, 0.0, qf)
        k_s[r, rows:2 * rows, :] = k_ref[:, cols].astype(F32)
        v_s[r, rows:2 * rows, :] = v_ref[:, cols].astype(F32)

    ones = jnp.ones((2 * rows, LANES), BF16)
    pick = (lax.broadcasted_iota(jnp.int32, (2 * rows, LANES), 0) % rows
            == lax.broadcasted_iota(jnp.int32, (2 * rows, LANES), 1))
    pick = jnp.where(pick, 1.0, 0.0).astype(BF16)
    nt = (((1,), (1,)), ((), ()))

    def attend(q0_pieces, q1_pieces, k_pieces, v_pieces, bias):
        q2 = jnp.concatenate(q0_pieces + q1_pieces, axis=0).astype(BF16)
        k2 = jnp.concatenate(k_pieces, axis=0).astype(BF16)
        v2 = jnp.concatenate(v_pieces, axis=0).astype(BF16)
        s = lax.dot_general(jnp.concatenate([q2, pick], axis=1), jnp.concatenate([k2, bias], axis=1), nt,
                            preferred_element_type=F32)
        m = jnp.max(s, axis=-1, keepdims=True)
        p = jnp.exp2(s - m).astype(BF16)
        pv = jnp.dot(p, jnp.concatenate([v2, ones], axis=1), preferred_element_type=F32)
        return (jnp.where(low, pv[:rows, :LANES], pv[rows:, :LANES]),
                jnp.where(low, pv[:rows, LANES:], pv[rows:, LANES:]),
                jnp.where(low, m[:rows], m[rows:]))

    def variant(cond):
        return jnp.where(cond, 1, 0)

    def store(slot, pieces, results):
        for res, rows_dst, rows_src in pieces:
            for ref, val in zip((num_s, den_s, max_s), results):
                ref[slot, res, rows_dst, :] = val[rows_src]

    for c in range(4):
        for j in range(4):
            qrow, krow = slice(32 * j, 32 * j + 32), slice(96 + 32 * j, 160 + 32 * j)
            res = attend([q0_s[4 * a + c, qrow, :] for a in range(4)],
                         [q1_s[4 * a + c, qrow, :] for a in range(4)],
                         [k_s[4 * a + c, krow, :] for a in range(4)],
                         [v_s[4 * a + c, krow, :] for a in range(4)],
                         bias_ref[2 + variant(first_tile)] if j == 0 else bias_ref[2])
            store(0, [(4 * a + c, qrow, slice(32 * a, 32 * a + 32)) for a in range(4)], res)

    for j in range(MAX_DIL):
        qrow, krow = slice(8 * j, 8 * j + 8), slice(120 + 8 * j, 136 + 8 * j)
        res = attend([q0_s[r, qrow, :] for r in range(MAX_DIL)],
                     [q1_s[r, qrow, :] for r in range(MAX_DIL)],
                     [k_s[r, krow, :] for r in range(MAX_DIL)],
                     [v_s[r, krow, :] for r in range(MAX_DIL)],
                     bias_ref[4 + variant(first_tile)] if j == 0 else bias_ref[4])
        store(1, [(r, qrow, slice(8 * r, 8 * r + 8)) for r in range(MAX_DIL)], res)

    for r in range(MAX_DIL):
        n0, d0, m0 = attend([q0_s[r]], [q1_s[r]], [k_s[r]], [v_s[r]], bias_ref[variant(first_tile)])
        m1, m2 = max_s[0, r], max_s[1, r]
        top = jnp.maximum(jnp.maximum(m0, m1), m2)
        e0, e1, e2 = jnp.exp2(m0 - top), jnp.exp2(m1 - top), jnp.exp2(m2 - top)
        num = e0 * n0 + e1 * num_s[0, r] + e2 * num_s[1, r]
        den = e0 * d0 + e1 * den_s[0, r] + e2 * den_s[1, r]
        o_ref[:, r * LANES:(r + 1) * LANES] = (num * (1.0 / den)).astype(BF16)
        k_s[r, 0:rows, :] = k_s[r, rows:2 * rows, :]
        v_s[r, 0:rows, :] = v_s[r, rows:2 * rows, :]


def _dilated_attention(qa, ka, va):
    b, n_pairs, view_len, width = qa.shape
    tile = pl.BlockSpec((None, None, ATTN_ROWS, width), lambda bi, hp, i: (bi, hp, i, 0))
    biases = _attention_biases()
    scratch = lambda n: pltpu.VMEM((MAX_DIL, n, LANES), F32)
    return pl.pallas_call(
        _attention_kernel,
        grid=(b, n_pairs, view_len // ATTN_ROWS),
        in_specs=[pl.BlockSpec(biases.shape, lambda bi, hp, i: (0, 0, 0)), tile, tile, tile],
        out_specs=tile,
        out_shape=jax.ShapeDtypeStruct(qa.shape, BF16),
        scratch_shapes=[scratch(ATTN_ROWS), scratch(ATTN_ROWS), scratch(2 * ATTN_ROWS), scratch(2 * ATTN_ROWS)]
        + [pltpu.VMEM((len(DILATIONS) - 1, MAX_DIL, ATTN_ROWS, LANES), F32)] * 3,
        compiler_params=pltpu.CompilerParams(dimension_semantics=("arbitrary", "arbitrary", "arbitrary"),
                                             vmem_limit_bytes=VMEM_LIMIT_BYTES),
        name="dilated_attention",
    )(biases, qa, ka, va)


def _outproj_ffn_kernel(x_ref, attn_ref, ret_ref, gpost_ref, gpre_ref, gffn_ref, wout_ref, wup_ref, cw_ref, cb_ref,
                        wdown_ref, o_ref, act_s, u_s, halo_s, stage_s):
    tm = x_ref.shape[0]
    d_ff = wdown_ref.shape[0]

    @pl.when(pl.program_id(1) == 0)
    def _():
        halo_s[...] = jnp.zeros_like(halo_s)

    for hp in range(N_PAIRS):
        for r in range(MAX_DIL):
            piece = attn_ref[hp, :, r * LANES:(r + 1) * LANES].astype(F32)
            stage_s[hp, pl.ds(r, tm // MAX_DIL, stride=MAX_DIL), :] = piece
    mixed = jnp.concatenate([stage_s[hp].astype(BF16) for hp in range(N_PAIRS)] + [ret_ref[...]],
                            axis=-1)
    mix = jnp.dot(mixed, wout_ref[...], preferred_element_type=F32)
    x1 = x_ref[...] + _rms(mix, gpost_ref[...])
    hn = _rms(x1, gpre_ref[...]).astype(BF16)

    def conv(c, part, cols, u):
        ub = u_s.at[2 * (c % 2) + part]
        ub[0:SUBLANES, :] = halo_s[:, cols]
        ub[SUBLANES:SUBLANES + tm, :] = u
        halo_s[:, cols] = u[tm - SUBLANES:tm]
        return (cb_ref[:, cols] + cw_ref[2:3, cols] * u
                + cw_ref[1:2, cols] * ub[SUBLANES - 1:SUBLANES - 1 + tm, :]
                + cw_ref[0:1, cols] * ub[SUBLANES - 2:SUBLANES - 2 + tm, :])

    for c in range(d_ff // FF_CHUNK):
        gcols = slice(c * FF_CHUNK, (c + 1) * FF_CHUNK)
        ucols = slice(d_ff + c * FF_CHUNK, d_ff + (c + 1) * FF_CHUNK)
        gate = conv(c, 0, gcols, jnp.dot(hn, wup_ref[:, gcols], preferred_element_type=F32))
        half_up = conv(c, 1, ucols, jnp.dot(hn, wup_ref[:, ucols], preferred_element_type=F32))
        inner = gate * (GELU_C1 + GELU_C3 * (gate * gate))
        act_s[:, gcols] = (gate * (1.0 + jnp.tanh(inner)) * half_up).astype(BF16)

    ffn = jnp.dot(act_s[...], wdown_ref[...], preferred_element_type=F32)
    o_ref[...] = x1 + _rms(ffn, gffn_ref[...])


def _outproj_ffn(x, attn, ret, g_post, g_pre, g_ffn, w_out_bf, w_up_bf, conv_w, conv_b, w_down_bf):
    b, s, d = x.shape
    tm = TOKEN_TILE
    d_ff = w_down_bf.shape[0]
    const = lambda shape: pl.BlockSpec(shape, lambda bi, i: (0,) * len(shape), pipeline_mode=pl.Buffered(1))
    return pl.pallas_call(
        _outproj_ffn_kernel,
        grid=(b, s // tm),
        in_specs=[
            pl.BlockSpec((None, tm, d), lambda bi, i: (bi, i, 0)),
            pl.BlockSpec((None, N_PAIRS, tm // MAX_DIL, MAX_DIL * LANES), lambda bi, i: (bi, 0, i, 0)),
            pl.BlockSpec((None, tm, RET_V_WIDTH), lambda bi, i: (bi, i, 0)),
            const((1, d)), const((1, d)), const((1, d)),
            const(w_out_bf.shape), const(w_up_bf.shape), const(conv_w.shape), const(conv_b.shape),
            const(w_down_bf.shape),
        ],
        out_specs=pl.BlockSpec((None, tm, d), lambda bi, i: (bi, i, 0)),
        out_shape=jax.ShapeDtypeStruct((b, s, d), F32),
        scratch_shapes=[pltpu.VMEM((tm, d_ff), BF16),
                        pltpu.VMEM((4, tm + SUBLANES, FF_CHUNK), F32),
                        pltpu.VMEM((SUBLANES, 2 * d_ff), F32),
                        pltpu.VMEM((N_PAIRS, tm, LANES), F32)],
        compiler_params=pltpu.CompilerParams(dimension_semantics=("arbitrary", "arbitrary"),
                                             vmem_limit_bytes=VMEM_LIMIT_BYTES),
        name="outproj_ffn",
    )(x, attn, ret, g_post, g_pre, g_ffn, w_out_bf, w_up_bf, conv_w, conv_b, w_down_bf)


def _layer(x, g_mix_pre, g_mix_post, w_in, w_out, g_ffn_pre, g_ffn_post, w_up, conv_w, conv_b, w_down):
    d = x.shape[-1]
    d_ff = w_down.shape[0]
    col_scale = np.ones((w_in.shape[1],), np.float32)
    col_scale[:ATTN_WIDTH] = ATTN_HEAD_DIM ** -0.5 * np.log2(np.e)
    kr0 = 3 * ATTN_WIDTH + RET_QK_WIDTH
    col_scale[kr0:kr0 + RET_QK_WIDTH] = RET_QK_DIM ** -0.5
    w_in_bf = (w_in * col_scale).astype(BF16)
    qa, ka, va, ret = _inproj_retention(x, g_mix_pre.reshape(1, d), w_in_bf)
    attn = _dilated_attention(qa, ka, va)
    up_half = np.concatenate([np.ones((d_ff,), np.float32), np.full((d_ff,), 0.5, np.float32)])
    return _outproj_ffn(x, attn, ret, g_mix_post.reshape(1, d), g_ffn_pre.reshape(1, d), g_ffn_post.reshape(1, d),
                        w_out.astype(BF16), w_up.astype(BF16), conv_w * up_half, conv_b.reshape(1, -1) * up_half,
                        w_down.astype(BF16))


def kernel(x, mix_pre_gain, mix_post_gain, w_in, w_out, ffn_pre_gain, ffn_post_gain, w_up, conv_w, conv_b, w_down):
    for layer in range(mix_pre_gain.shape[0]):
        x = _layer(x, mix_pre_gain[layer], mix_post_gain[layer], w_in[layer], w_out[layer], ffn_pre_gain[layer],
                   ffn_post_gain[layer], w_up[layer], conv_w[layer], conv_b[layer], w_down[layer])
    return x
```

```python
import numpy as np
import jax
import jax.numpy as jnp
from jax import lax
from jax.experimental import pallas as pl
from jax.experimental.pallas import tpu as pltpu

F32 = jnp.float32
BF16 = jnp.bfloat16

ATTN_HEAD_DIM = 64
N_ATTN_HEADS = 8
ATTN_WIDTH = N_ATTN_HEADS * ATTN_HEAD_DIM
ATTN_STEPS = 128
DILATIONS = (1, 4, 16)
RET_V_DIM = 128
N_RET_HEADS = 4
RET_QK_DIM = 64
RET_QK_WIDTH = N_RET_HEADS * RET_QK_DIM
RET_V_WIDTH = N_RET_HEADS * RET_V_DIM
ROPE_BASE = 10000.0
CONV_WIDTH = 3
NORM_EPS = 1e-6
MASK_VALUE = -1e30

LANES = 128
SUBLANES = 8
HEAD_PAIR = LANES // ATTN_HEAD_DIM
N_PAIRS = N_ATTN_HEADS // HEAD_PAIR
MAX_DIL = max(DILATIONS)
ATTN_ROWS = 128
TOKEN_TILE = 512
INPROJ_TILE = 1024
RET_CHUNK = 256
FF_CHUNK = 256
FF_STREAMS = 2
GELU_C1 = float(np.sqrt(2.0 / np.pi).astype(np.float32))
GELU_C3 = float(np.float32(GELU_C1 * 0.044715))
VMEM_LIMIT_BYTES = 56 * 1024 * 1024


def _rms(x, gain):
    return x * lax.rsqrt(jnp.mean(x * x, axis=-1, keepdims=True) + NORM_EPS) * gain


def _inproj_retention_kernel(x_ref, gain_ref, w_ref, cos_ref, sin_ref, dmat_ref, qdec_ref, kdec_ref, sdec_ref,
                             qa_ref, ka_ref, va_ref, ret_ref, state_ref, stage_ref):
    @pl.when(pl.program_id(1) == 0)
    def _():
        state_ref[...] = jnp.zeros_like(state_ref)

    base = 3 * ATTN_WIDTH
    vbase = base + 2 * RET_QK_WIDTH
    gbase = vbase + RET_V_WIDTH
    lane = lax.broadcasted_iota(jnp.int32, (1, RET_QK_WIDTH), 1)
    first_half = (lane % RET_QK_DIM) < (RET_QK_DIM // 2)
    lane128 = lax.broadcasted_iota(jnp.int32, (1, LANES), 1)
    low_lanes = lane128 < RET_QK_DIM
    row128 = lax.broadcasted_iota(jnp.int32, (LANES, 1), 0)
    low_rows = row128 < RET_QK_DIM
    nt = (((1,), (1,)), ((), ()))
    tn = (((0,), (0,)), ((), ()))
    view_rows = RET_CHUNK // MAX_DIL
    pitch = stage_ref.shape[1] // MAX_DIL

    for c in range(x_ref.shape[0] // RET_CHUNK):
        rows = slice(c * RET_CHUNK, (c + 1) * RET_CHUNK)
        h = _rms(x_ref[rows, :], gain_ref[...]).astype(BF16)
        proj = jnp.dot(h, w_ref[...], preferred_element_type=F32)

        for t, out_ref in enumerate((qa_ref, ka_ref, va_ref)):
            for hp in range(N_PAIRS):
                col = t * N_PAIRS + hp
                block = proj[:, col * LANES:(col + 1) * LANES]
                for g in range(RET_CHUNK // SUBLANES):
                    m = c * view_rows + g * SUBLANES // MAX_DIL
                    r0 = g * SUBLANES % MAX_DIL
                    stage_ref[col, pl.ds(r0 * pitch + m, SUBLANES, stride=pitch), :] = (
                        block[g * SUBLANES:(g + 1) * SUBLANES])
                for r in range(MAX_DIL):
                    piece = stage_ref[col, r * pitch + c * view_rows:r * pitch + (c + 1) * view_rows, :]
                    out_ref[hp, c * view_rows:(c + 1) * view_rows, r * LANES:(r + 1) * LANES] = piece.astype(BF16)

        def rotary(t):
            swapped = jnp.where(first_half,
                                pltpu.roll(t, RET_QK_WIDTH - RET_QK_DIM // 2, axis=1),
                                pltpu.roll(t, RET_QK_DIM // 2, axis=1))
            return t * cos_ref[rows, :] + swapped * sin_ref[rows, :]

        qr = rotary(proj[:, base:base + RET_QK_WIDTH])
        kr = rotary(proj[:, base + RET_QK_WIDTH:base + 2 * RET_QK_WIDTH])

        for p in range(N_RET_HEADS // 2):
            cols = slice(p * LANES, (p + 1) * LANES)
            q2 = qr[:, cols]
            k2 = kr[:, cols]
            q_heads = (jnp.where(low_lanes, q2, 0.0), jnp.where(low_lanes, 0.0, q2))
            lhs = jnp.concatenate(q_heads, axis=0).astype(BF16)
            scores = lax.dot_general(lhs, k2.astype(BF16), nt, preferred_element_type=F32)
            kdec = (k2 * kdec_ref[:, cols]).astype(BF16)
            state = state_ref[p]
            state_bf = state.astype(BF16)
            kv = []
            for hh in range(2):
                head = 2 * p + hh
                v_h = proj[:, vbase + head * RET_V_DIM:vbase + (head + 1) * RET_V_DIM].astype(BF16)
                s_h = (scores[hh * RET_CHUNK:(hh + 1) * RET_CHUNK] * dmat_ref[head]).astype(BF16)
                inner = jnp.dot(s_h, v_h, preferred_element_type=F32)
                q_dec = (q_heads[hh] * qdec_ref[:, cols]).astype(BF16)
                cross = jnp.dot(q_dec, state_bf, preferred_element_type=F32)
                o = inner + cross
                o = o * lax.rsqrt(jnp.mean(o * o, axis=-1, keepdims=True) + NORM_EPS)
                g = proj[:, gbase + head * RET_V_DIM:gbase + (head + 1) * RET_V_DIM]
                o = o * (g * (1.0 / (1.0 + jnp.exp(-g))))
                ret_ref[rows, head * RET_V_DIM:(head + 1) * RET_V_DIM] = o.astype(BF16)
                kv.append(lax.dot_general(kdec, v_h, tn, preferred_element_type=F32))
            state_ref[p] = sdec_ref[p] * state + jnp.where(low_rows, kv[0], kv[1])


def _retention_constants(seq):
    freqs = ROPE_BASE ** (-np.arange(0, RET_QK_DIM, 2, dtype=np.float64) / RET_QK_DIM)
    ang = np.arange(seq, dtype=np.float64)[:, None] * freqs[None]
    cos_h = np.concatenate([np.cos(ang), np.cos(ang)], axis=1)
    sin_h = np.concatenate([-np.sin(ang), np.sin(ang)], axis=1)
    cos_t = np.tile(cos_h, (1, N_RET_HEADS)).astype(np.float32)
    sin_t = np.tile(sin_h, (1, N_RET_HEADS)).astype(np.float32)
    log_gamma = np.log1p(-np.exp2(-5.0 - np.arange(N_RET_HEADS, dtype=np.float64)))
    idx = np.arange(RET_CHUNK, dtype=np.float64)
    rel = idx[:, None] - idx[None, :]
    dmat = np.where(rel >= 0, np.exp(log_gamma[:, None, None] * np.maximum(rel, 0.0)), 0.0)
    qdec = np.exp(log_gamma[None, :] * (idx + 1.0)[:, None])
    kdec = np.exp(log_gamma[None, :] * (RET_CHUNK - 1 - idx)[:, None])
    qdec = np.repeat(qdec, RET_QK_DIM, axis=1)
    kdec = np.repeat(kdec, RET_QK_DIM, axis=1)
    sdec = np.repeat(np.exp(log_gamma * RET_CHUNK), RET_QK_DIM).reshape(N_RET_HEADS // 2, LANES, 1)
    sdec = np.broadcast_to(sdec, (N_RET_HEADS // 2, LANES, LANES))
    f = lambda a: jnp.asarray(np.ascontiguousarray(a), dtype=F32)
    return f(cos_t), f(sin_t), f(dmat), f(qdec), f(kdec), f(sdec)


def _inproj_retention(x, gain, w_in_bf):
    b, s, d = x.shape
    tm = INPROJ_TILE
    in_width = w_in_bf.shape[1]
    cos_t, sin_t, dmat, qdec, kdec, sdec = _retention_constants(s)
    const = lambda shape: pl.BlockSpec(shape, lambda bi, i: (0,) * len(shape))
    pair_spec = pl.BlockSpec((None, N_PAIRS, tm // MAX_DIL, MAX_DIL * LANES), lambda bi, i: (bi, 0, i, 0))
    pair_shape = jax.ShapeDtypeStruct((b, N_PAIRS, s // MAX_DIL, MAX_DIL * LANES), BF16)
    return pl.pallas_call(
        _inproj_retention_kernel,
        grid=(b, s // tm),
        in_specs=[
            pl.BlockSpec((None, tm, d), lambda bi, i: (bi, i, 0)),
            const((1, d)),
            const((d, in_width)),
            pl.BlockSpec((tm, RET_QK_WIDTH), lambda bi, i: (i, 0)),
            pl.BlockSpec((tm, RET_QK_WIDTH), lambda bi, i: (i, 0)),
            const((N_RET_HEADS, RET_CHUNK, RET_CHUNK)),
            const((RET_CHUNK, RET_QK_WIDTH)),
            const((RET_CHUNK, RET_QK_WIDTH)),
            const((N_RET_HEADS // 2, LANES, LANES)),
        ],
        out_specs=[pair_spec, pair_spec, pair_spec,
                   pl.BlockSpec((None, tm, RET_V_WIDTH), lambda bi, i: (bi, i, 0))],
        out_shape=[pair_shape, pair_shape, pair_shape, jax.ShapeDtypeStruct((b, s, RET_V_WIDTH), BF16)],
        scratch_shapes=[pltpu.VMEM((N_RET_HEADS // 2, LANES, LANES), F32),
                        pltpu.VMEM((3 * N_PAIRS, MAX_DIL * (tm // MAX_DIL + SUBLANES), LANES), F32)],
        compiler_params=pltpu.CompilerParams(dimension_semantics=("arbitrary", "arbitrary"),
                                             vmem_limit_bytes=VMEM_LIMIT_BYTES),
        name="inproj_retention",
    )(x, gain, w_in_bf, cos_t, sin_t, dmat, qdec, kdec, sdec)


def _attention_biases():
    q = np.arange(ATTN_ROWS)[None, :]
    k = np.arange(2 * ATTN_ROWS)[:, None]
    out = []
    out += [(ATTN_ROWS + q - k, k >= ATTN_ROWS)]
    out += [(4 * (32 + q % 32 - k % 64) + q // 32 - k // 64, k % 64 >= 32)]
    out += [(16 * (8 + q % 8 - k % 16) + q // 8 - k // 16, k % 16 >= 8)]
    biases = []
    for dist, in_seq in out:
        band = (dist >= 0) & (dist <= ATTN_STEPS)
        biases.append(np.where(band, 0.0, MASK_VALUE))
        biases.append(np.where(band & in_seq, 0.0, MASK_VALUE))
    return jnp.asarray(np.stack(biases), dtype=BF16)


def _attention_kernel(bias_ref, q_ref, k_ref, v_ref, o_ref, q0_s, q1_s, k_s, v_s, num_s, den_s, max_s):
    first_tile = pl.program_id(2) == 0
    rows = ATTN_ROWS

    @pl.when(first_tile)
    def _():
        k_s[:, 0:rows, :] = jnp.zeros((MAX_DIL, rows, LANES), F32)
        v_s[:, 0:rows, :] = jnp.zeros((MAX_DIL, rows, LANES), F32)

    lane = lax.broadcasted_iota(jnp.int32, (1, LANES), 1)
    low = lane < ATTN_HEAD_DIM
    for r in range(MAX_DIL):
        cols = slice(r * LANES, (r + 1) * LANES)
        qf = q_ref[:, cols].astype(F32)
        q0_s[r] = jnp.where(low, qf, 0.0)
        q1_s[r] = jnp.where(low, 0.0, qf)
        k_s[r, rows:2 * rows, :] = k_ref[:, cols].astype(F32)
        v_s[r, rows:2 * rows, :] = v_ref[:, cols].astype(F32)

    ones = jnp.ones((2 * rows, LANES), BF16)
    pick = (lax.broadcasted_iota(jnp.int32, (2 * rows, LANES), 0) % rows
            == lax.broadcasted_iota(jnp.int32, (2 * rows, LANES), 1))
    pick = jnp.where(pick, 1.0, 0.0).astype(BF16)
    nt = (((1,), (1,)), ((), ()))

    def attend(q0_pieces, q1_pieces, k_pieces, v_pieces, bias):
        q2 = jnp.concatenate(q0_pieces + q1_pieces, axis=0).astype(BF16)
        k2 = jnp.concatenate(k_pieces, axis=0).astype(BF16)
        v2 = jnp.concatenate(v_pieces, axis=0).astype(BF16)
        s = lax.dot_general(jnp.concatenate([q2, pick], axis=1), jnp.concatenate([k2, bias], axis=1), nt,
                            preferred_element_type=F32)
        m = jnp.max(s, axis=-1, keepdims=True)
        p = jnp.exp2(s - m).astype(BF16)
        pv = jnp.dot(p, jnp.concatenate([v2, ones], axis=1), preferred_element_type=F32)
        return (jnp.where(low, pv[:rows, :LANES], pv[rows:, :LANES]),
                jnp.where(low, pv[:rows, LANES:], pv[rows:, LANES:]),
                jnp.where(low, m[:rows], m[rows:]))

    def variant(cond):
        return jnp.where(cond, 1, 0)

    def store(slot, pieces, results):
        for res, rows_dst, rows_src in pieces:
            for ref, val in zip((num_s, den_s, max_s), results):
                ref[slot, res, rows_dst, :] = val[rows_src]

    for c in range(4):
        for j in range(4):
            qrow, krow = slice(32 * j, 32 * j + 32), slice(96 + 32 * j, 160 + 32 * j)
            res = attend([q0_s[4 * a + c, qrow, :] for a in range(4)],
                         [q1_s[4 * a + c, qrow, :] for a in range(4)],
                         [k_s[4 * a + c, krow, :] for a in range(4)],
                         [v_s[4 * a + c, krow, :] for a in range(4)],
                         bias_ref[2 + variant(first_tile)] if j == 0 else bias_ref[2])
            store(0, [(4 * a + c, qrow, slice(32 * a, 32 * a + 32)) for a in range(4)], res)

    for j in range(MAX_DIL):
        qrow, krow = slice(8 * j, 8 * j + 8), slice(120 + 8 * j, 136 + 8 * j)
        res = attend([q0_s[r, qrow, :] for r in range(MAX_DIL)],
                     [q1_s[r, qrow, :] for r in range(MAX_DIL)],
                     [k_s[r, krow, :] for r in range(MAX_DIL)],
                     [v_s[r, krow, :] for r in range(MAX_DIL)],
                     bias_ref[4 + variant(first_tile)] if j == 0 else bias_ref[4])
        store(1, [(r, qrow, slice(8 * r, 8 * r + 8)) for r in range(MAX_DIL)], res)

    for r in range(MAX_DIL):
        n0, d0, m0 = attend([q0_s[r]], [q1_s[r]], [k_s[r]], [v_s[r]], bias_ref[variant(first_tile)])
        m1, m2 = max_s[0, r], max_s[1, r]
        top = jnp.maximum(jnp.maximum(m0, m1), m2)
        e0, e1, e2 = jnp.exp2(m0 - top), jnp.exp2(m1 - top), jnp.exp2(m2 - top)
        num = e0 * n0 + e1 * num_s[0, r] + e2 * num_s[1, r]
        den = e0 * d0 + e1 * den_s[0, r] + e2 * den_s[1, r]
        o_ref[:, r * LANES:(r + 1) * LANES] = (num * (1.0 / den)).astype(BF16)
        k_s[r, 0:rows, :] = k_s[r, rows:2 * rows, :]
        v_s[r, 0:rows, :] = v_s[r, rows:2 * rows, :]


def _dilated_attention(qa, ka, va):
    b, n_pairs, view_len, width = qa.shape
    tile = pl.BlockSpec((None, None, ATTN_ROWS, width), lambda bi, hp, i: (bi, hp, i, 0))
    biases = _attention_biases()
    scratch = lambda n: pltpu.VMEM((MAX_DIL, n, LANES), F32)
    return pl.pallas_call(
        _attention_kernel,
        grid=(b, n_pairs, view_len // ATTN_ROWS),
        in_specs=[pl.BlockSpec(biases.shape, lambda bi, hp, i: (0, 0, 0)), tile, tile, tile],
        out_specs=tile,
        out_shape=jax.ShapeDtypeStruct(qa.shape, BF16),
        scratch_shapes=[scratch(ATTN_ROWS), scratch(ATTN_ROWS), scratch(2 * ATTN_ROWS), scratch(2 * ATTN_ROWS)]
        + [pltpu.VMEM((len(DILATIONS) - 1, MAX_DIL, ATTN_ROWS, LANES), F32)] * 3,
        compiler_params=pltpu.CompilerParams(dimension_semantics=("arbitrary", "arbitrary", "arbitrary"),
                                             vmem_limit_bytes=VMEM_LIMIT_BYTES),
        name="dilated_attention",
    )(biases, qa, ka, va)


def _outproj_ffn_kernel(x_ref, attn_ref, ret_ref, gpost_ref, gpre_ref, gffn_ref, wout_ref, wup_ref, cw_ref, cb_ref,
                        wdown_ref, o_ref, act_s, halo_s, stage_s):
    tm = x_ref.shape[0]
    d_ff = wdown_ref.shape[0]

    @pl.when(pl.program_id(1) == 0)
    def _():
        halo_s[...] = jnp.zeros_like(halo_s)

    first_rows = lax.broadcasted_iota(jnp.int32, (SUBLANES, 1), 0)
    hm = tm // FF_STREAMS
    vm = hm // MAX_DIL
    x1, hn = [], []
    for st in range(FF_STREAMS):
        rows = slice(st * hm, (st + 1) * hm)
        for hp in range(N_PAIRS):
            for r in range(MAX_DIL):
                piece = attn_ref[hp, st * vm:(st + 1) * vm, r * LANES:(r + 1) * LANES].astype(F32)
                stage_s[hp, pl.ds(st * hm + r, vm, stride=MAX_DIL), :] = piece
        mixed = jnp.concatenate([stage_s[hp, rows, :].astype(BF16) for hp in range(N_PAIRS)] + [ret_ref[rows, :]],
                                axis=-1)
        mix = jnp.dot(mixed, wout_ref[...], preferred_element_type=F32)
        x1.append(x_ref[rows, :] + _rms(mix, gpost_ref[...]))
        hn.append(_rms(x1[st], gpre_ref[...]).astype(BF16))

    def conv(cols, u):
        halo = halo_s[:, cols]
        halo_s[:, cols] = u[hm - SUBLANES:hm]
        taps = []
        for shift in (1, 2):
            rolled = pltpu.roll(u, shift, axis=0)
            head = jnp.where(first_rows < shift, pltpu.roll(halo, shift, axis=0), rolled[0:SUBLANES])
            taps.append(jnp.concatenate([head, rolled[SUBLANES:]], axis=0))
        return cb_ref[:, cols] + cw_ref[2:3, cols] * u + cw_ref[1:2, cols] * taps[0] + cw_ref[0:1, cols] * taps[1]

    for c in range(d_ff // FF_CHUNK):
        gcols = slice(c * FF_CHUNK, (c + 1) * FF_CHUNK)
        ucols = slice(d_ff + c * FF_CHUNK, d_ff + (c + 1) * FF_CHUNK)
        for st in range(FF_STREAMS):
            gate = conv(gcols, jnp.dot(hn[st], wup_ref[:, gcols], preferred_element_type=F32))
            half_up = conv(ucols, jnp.dot(hn[st], wup_ref[:, ucols], preferred_element_type=F32))
            inner = gate * (GELU_C1 + GELU_C3 * (gate * gate))
            act_s[st * hm:(st + 1) * hm, gcols] = (gate * (1.0 + jnp.tanh(inner)) * half_up).astype(BF16)

    for st in range(FF_STREAMS):
        rows = slice(st * hm, (st + 1) * hm)
        ffn = jnp.dot(act_s[rows, :], wdown_ref[...], preferred_element_type=F32)
        o_ref[rows, :] = x1[st] + _rms(ffn, gffn_ref[...])


def _outproj_ffn(x, attn, ret, g_post, g_pre, g_ffn, w_out_bf, w_up_bf, conv_w, conv_b, w_down_bf):
    b, s, d = x.shape
    tm = TOKEN_TILE
    d_ff = w_down_bf.shape[0]
    const = lambda shape: pl.BlockSpec(shape, lambda bi, i: (0,) * len(shape), pipeline_mode=pl.Buffered(1))
    return pl.pallas_call(
        _outproj_ffn_kernel,
        grid=(b, s // tm),
        in_specs=[
            pl.BlockSpec((None, tm, d), lambda bi, i: (bi, i, 0)),
            pl.BlockSpec((None, N_PAIRS, tm // MAX_DIL, MAX_DIL * LANES), lambda bi, i: (bi, 0, i, 0)),
            pl.BlockSpec((None, tm, RET_V_WIDTH), lambda bi, i: (bi, i, 0)),
            const((1, d)), const((1, d)), const((1, d)),
            const(w_out_bf.shape), const(w_up_bf.shape), const(conv_w.shape), const(conv_b.shape),
            const(w_down_bf.shape),
        ],
        out_specs=pl.BlockSpec((None, tm, d), lambda bi, i: (bi, i, 0)),
        out_shape=jax.ShapeDtypeStruct((b, s, d), F32),
        scratch_shapes=[pltpu.VMEM((tm, d_ff), BF16),
                        pltpu.VMEM((SUBLANES, 2 * d_ff), F32),
                        pltpu.VMEM((N_PAIRS, tm, LANES), F32)],
        compiler_params=pltpu.CompilerParams(dimension_semantics=("arbitrary", "arbitrary"),
                                             vmem_limit_bytes=VMEM_LIMIT_BYTES),
        name="outproj_ffn",
    )(x, attn, ret, g_post, g_pre, g_ffn, w_out_bf, w_up_bf, conv_w, conv_b, w_down_bf)


def _layer(x, g_mix_pre, g_mix_post, w_in, w_out, g_ffn_pre, g_ffn_post, w_up, conv_w, conv_b, w_down):
    d = x.shape[-1]
    d_ff = w_down.shape[0]
    col_scale = np.ones((w_in.shape[1],), np.float32)
    col_scale[:ATTN_WIDTH] = ATTN_HEAD_DIM ** -0.5 * np.log2(np.e)
    kr0 = 3 * ATTN_WIDTH + RET_QK_WIDTH
    col_scale[kr0:kr0 + RET_QK_WIDTH] = RET_QK_DIM ** -0.5
    w_in_bf = (w_in * col_scale).astype(BF16)
    qa, ka, va, ret = _inproj_retention(x, g_mix_pre.reshape(1, d), w_in_bf)
    attn = _dilated_attention(qa, ka, va)
    up_half = np.concatenate([np.ones((d_ff,), np.float32), np.full((d_ff,), 0.5, np.float32)])
    return _outproj_ffn(x, attn, ret, g_mix_post.reshape(1, d), g_ffn_pre.reshape(1, d), g_ffn_post.reshape(1, d),
                        w_out.astype(BF16), w_up.astype(BF16), conv_w * up_half, conv_b.reshape(1, -1) * up_half,
                        w_down.astype(BF16))


def kernel(x, mix_pre_gain, mix_post_gain, w_in, w_out, ffn_pre_gain, ffn_post_gain, w_up, conv_w, conv_b, w_down):
    for layer in range(mix_pre_gain.shape[0]):
        x = _layer(x, mix_pre_gain[layer], mix_post_gain[layer], w_in[layer], w_out[layer], ffn_pre_gain[layer],
                   ffn_post_gain[layer], w_up[layer], conv_w[layer], conv_b[layer], w_down[layer])
    return x
```

```python
import numpy as np
import jax
import jax.numpy as jnp
from jax import lax
from jax.experimental import pallas as pl
from jax.experimental.pallas import tpu as pltpu

F32 = jnp.float32
BF16 = jnp.bfloat16

ATTN_HEAD_DIM = 64
N_ATTN_HEADS = 8
ATTN_WIDTH = N_ATTN_HEADS * ATTN_HEAD_DIM
ATTN_STEPS = 128
DILATIONS = (1, 4, 16)
RET_V_DIM = 128
N_RET_HEADS = 4
RET_QK_DIM = 64
RET_QK_WIDTH = N_RET_HEADS * RET_QK_DIM
RET_V_WIDTH = N_RET_HEADS * RET_V_DIM
ROPE_BASE = 10000.0
CONV_WIDTH = 3
NORM_EPS = 1e-6
MASK_VALUE = -1e30

LANES = 128
SUBLANES = 8
HEAD_PAIR = LANES // ATTN_HEAD_DIM
N_PAIRS = N_ATTN_HEADS // HEAD_PAIR
MAX_DIL = max(DILATIONS)
ATTN_ROWS = 128
TOKEN_TILE = 512
INPROJ_TILE = 1024
RET_CHUNK = 256
ATTN_PAIRS = 2
FF_CHUNK = 256
FF_STREAMS = 2
GELU_C1 = float(np.sqrt(2.0 / np.pi).astype(np.float32))
GELU_C3 = float(np.float32(GELU_C1 * 0.044715))
VMEM_LIMIT_BYTES = 56 * 1024 * 1024


def _rms(x, gain):
    return x * lax.rsqrt(jnp.mean(x * x, axis=-1, keepdims=True) + NORM_EPS) * gain


def _inproj_retention_kernel(x_ref, gain_ref, w_ref, cos_ref, sin_ref, dmat_ref, qdec_ref, kdec_ref, sdec_ref,
                             qa_ref, ka_ref, va_ref, ret_ref, state_ref, stage_ref):
    @pl.when(pl.program_id(1) == 0)
    def _():
        state_ref[...] = jnp.zeros_like(state_ref)

    base = 3 * ATTN_WIDTH
    vbase = base + 2 * RET_QK_WIDTH
    gbase = vbase + RET_V_WIDTH
    lane = lax.broadcasted_iota(jnp.int32, (1, RET_QK_WIDTH), 1)
    first_half = (lane % RET_QK_DIM) < (RET_QK_DIM // 2)
    lane128 = lax.broadcasted_iota(jnp.int32, (1, LANES), 1)
    low_lanes = lane128 < RET_QK_DIM
    row128 = lax.broadcasted_iota(jnp.int32, (LANES, 1), 0)
    low_rows = row128 < RET_QK_DIM
    nt = (((1,), (1,)), ((), ()))
    tn = (((0,), (0,)), ((), ()))
    view_rows = RET_CHUNK // MAX_DIL
    pitch = stage_ref.shape[1] // MAX_DIL

    for c in range(x_ref.shape[0] // RET_CHUNK):
        rows = slice(c * RET_CHUNK, (c + 1) * RET_CHUNK)
        h = _rms(x_ref[rows, :], gain_ref[...]).astype(BF16)
        proj = jnp.dot(h, w_ref[...], preferred_element_type=F32)

        for t, out_ref in enumerate((qa_ref, ka_ref, va_ref)):
            for hp in range(N_PAIRS):
                col = t * N_PAIRS + hp
                block = proj[:, col * LANES:(col + 1) * LANES]
                for g in range(RET_CHUNK // SUBLANES):
                    m = c * view_rows + g * SUBLANES // MAX_DIL
                    r0 = g * SUBLANES % MAX_DIL
                    stage_ref[col, pl.ds(r0 * pitch + m, SUBLANES, stride=pitch), :] = (
                        block[g * SUBLANES:(g + 1) * SUBLANES])
                for r in range(MAX_DIL):
                    piece = stage_ref[col, r * pitch + c * view_rows:r * pitch + (c + 1) * view_rows, :]
                    out_ref[hp, c * view_rows:(c + 1) * view_rows, r * LANES:(r + 1) * LANES] = piece.astype(BF16)

        def rotary(t):
            swapped = jnp.where(first_half,
                                pltpu.roll(t, RET_QK_WIDTH - RET_QK_DIM // 2, axis=1),
                                pltpu.roll(t, RET_QK_DIM // 2, axis=1))
            return t * cos_ref[rows, :] + swapped * sin_ref[rows, :]

        qr = rotary(proj[:, base:base + RET_QK_WIDTH])
        kr = rotary(proj[:, base + RET_QK_WIDTH:base + 2 * RET_QK_WIDTH])

        for p in range(N_RET_HEADS // 2):
            cols = slice(p * LANES, (p + 1) * LANES)
            q2 = qr[:, cols]
            k2 = kr[:, cols]
            q_heads = (jnp.where(low_lanes, q2, 0.0), jnp.where(low_lanes, 0.0, q2))
            lhs = jnp.concatenate(q_heads, axis=0).astype(BF16)
            scores = lax.dot_general(lhs, k2.astype(BF16), nt, preferred_element_type=F32)
            kdec = (k2 * kdec_ref[:, cols]).astype(BF16)
            state = state_ref[p]
            state_bf = state.astype(BF16)
            kv = []
            for hh in range(2):
                head = 2 * p + hh
                v_h = proj[:, vbase + head * RET_V_DIM:vbase + (head + 1) * RET_V_DIM].astype(BF16)
                s_h = (scores[hh * RET_CHUNK:(hh + 1) * RET_CHUNK] * dmat_ref[head]).astype(BF16)
                inner = jnp.dot(s_h, v_h, preferred_element_type=F32)
                q_dec = (q_heads[hh] * qdec_ref[:, cols]).astype(BF16)
                cross = jnp.dot(q_dec, state_bf, preferred_element_type=F32)
                o = inner + cross
                o = o * lax.rsqrt(jnp.mean(o * o, axis=-1, keepdims=True) + NORM_EPS)
                g = proj[:, gbase + head * RET_V_DIM:gbase + (head + 1) * RET_V_DIM]
                o = o * (g * (1.0 / (1.0 + jnp.exp(-g))))
                ret_ref[rows, head * RET_V_DIM:(head + 1) * RET_V_DIM] = o.astype(BF16)
                kv.append(lax.dot_general(kdec, v_h, tn, preferred_element_type=F32))
            state_ref[p] = sdec_ref[p] * state + jnp.where(low_rows, kv[0], kv[1])


def _retention_constants(seq):
    freqs = ROPE_BASE ** (-np.arange(0, RET_QK_DIM, 2, dtype=np.float64) / RET_QK_DIM)
    ang = np.arange(seq, dtype=np.float64)[:, None] * freqs[None]
    cos_h = np.concatenate([np.cos(ang), np.cos(ang)], axis=1)
    sin_h = np.concatenate([-np.sin(ang), np.sin(ang)], axis=1)
    cos_t = np.tile(cos_h, (1, N_RET_HEADS)).astype(np.float32)
    sin_t = np.tile(sin_h, (1, N_RET_HEADS)).astype(np.float32)
    log_gamma = np.log1p(-np.exp2(-5.0 - np.arange(N_RET_HEADS, dtype=np.float64)))
    idx = np.arange(RET_CHUNK, dtype=np.float64)
    rel = idx[:, None] - idx[None, :]
    dmat = np.where(rel >= 0, np.exp(log_gamma[:, None, None] * np.maximum(rel, 0.0)), 0.0)
    qdec = np.exp(log_gamma[None, :] * (idx + 1.0)[:, None])
    kdec = np.exp(log_gamma[None, :] * (RET_CHUNK - 1 - idx)[:, None])
    qdec = np.repeat(qdec, RET_QK_DIM, axis=1)
    kdec = np.repeat(kdec, RET_QK_DIM, axis=1)
    sdec = np.repeat(np.exp(log_gamma * RET_CHUNK), RET_QK_DIM).reshape(N_RET_HEADS // 2, LANES, 1)
    sdec = np.broadcast_to(sdec, (N_RET_HEADS // 2, LANES, LANES))
    f = lambda a: jnp.asarray(np.ascontiguousarray(a), dtype=F32)
    return f(cos_t), f(sin_t), f(dmat), f(qdec), f(kdec), f(sdec)


def _inproj_retention(x, gain, w_in_bf):
    b, s, d = x.shape
    tm = INPROJ_TILE
    in_width = w_in_bf.shape[1]
    cos_t, sin_t, dmat, qdec, kdec, sdec = _retention_constants(s)
    const = lambda shape: pl.BlockSpec(shape, lambda bi, i: (0,) * len(shape))
    pair_spec = pl.BlockSpec((None, N_PAIRS, tm // MAX_DIL, MAX_DIL * LANES), lambda bi, i: (bi, 0, i, 0))
    pair_shape = jax.ShapeDtypeStruct((b, N_PAIRS, s // MAX_DIL, MAX_DIL * LANES), BF16)
    return pl.pallas_call(
        _inproj_retention_kernel,
        grid=(b, s // tm),
        in_specs=[
            pl.BlockSpec((None, tm, d), lambda bi, i: (bi, i, 0)),
            const((1, d)),
            const((d, in_width)),
            pl.BlockSpec((tm, RET_QK_WIDTH), lambda bi, i: (i, 0)),
            pl.BlockSpec((tm, RET_QK_WIDTH), lambda bi, i: (i, 0)),
            const((N_RET_HEADS, RET_CHUNK, RET_CHUNK)),
            const((RET_CHUNK, RET_QK_WIDTH)),
            const((RET_CHUNK, RET_QK_WIDTH)),
            const((N_RET_HEADS // 2, LANES, LANES)),
        ],
        out_specs=[pair_spec, pair_spec, pair_spec,
                   pl.BlockSpec((None, tm, RET_V_WIDTH), lambda bi, i: (bi, i, 0))],
        out_shape=[pair_shape, pair_shape, pair_shape, jax.ShapeDtypeStruct((b, s, RET_V_WIDTH), BF16)],
        scratch_shapes=[pltpu.VMEM((N_RET_HEADS // 2, LANES, LANES), F32),
                        pltpu.VMEM((3 * N_PAIRS, MAX_DIL * (tm // MAX_DIL + SUBLANES), LANES), F32)],
        compiler_params=pltpu.CompilerParams(dimension_semantics=("arbitrary", "arbitrary"),
                                             vmem_limit_bytes=VMEM_LIMIT_BYTES),
        name="inproj_retention",
    )(x, gain, w_in_bf, cos_t, sin_t, dmat, qdec, kdec, sdec)


def _attention_biases():
    q = np.arange(ATTN_ROWS)[None, :]
    k = np.arange(2 * ATTN_ROWS)[:, None]
    out = []
    out += [(ATTN_ROWS + q - k, k >= ATTN_ROWS)]
    out += [(4 * (32 + q % 32 - k % 64) + q // 32 - k // 64, k % 64 >= 32)]
    out += [(16 * (8 + q % 8 - k % 16) + q // 8 - k // 16, k % 16 >= 8)]
    biases = []
    for dist, in_seq in out:
        band = (dist >= 0) & (dist <= ATTN_STEPS)
        biases.append(np.where(band, 0.0, MASK_VALUE))
        biases.append(np.where(band & in_seq, 0.0, MASK_VALUE))
    return jnp.asarray(np.stack(biases), dtype=BF16)


def _attention_kernel(bias_ref, q_ref, k_ref, v_ref, o_ref, *scratch):
    first_tile = pl.program_id(2) == 0
    for hs in range(ATTN_PAIRS):
        _attention_pair(first_tile, bias_ref, q_ref.at[hs], k_ref.at[hs], v_ref.at[hs], o_ref.at[hs],
                        *(ref.at[hs] for ref in scratch))


def _attention_pair(first_tile, bias_ref, q_ref, k_ref, v_ref, o_ref, q0_s, q1_s, k_s, v_s, num_s, den_s, max_s):
    rows = ATTN_ROWS

    @pl.when(first_tile)
    def _():
        k_s[:, 0:rows, :] = jnp.zeros((MAX_DIL, rows, LANES), F32)
        v_s[:, 0:rows, :] = jnp.zeros((MAX_DIL, rows, LANES), F32)

    lane = lax.broadcasted_iota(jnp.int32, (1, LANES), 1)
    low = lane < ATTN_HEAD_DIM
    for r in range(MAX_DIL):
        cols = slice(r * LANES, (r + 1) * LANES)
        qf = q_ref[:, cols].astype(F32)
        q0_s[r] = jnp.where(low, qf, 0.0)
        q1_s[r] = jnp.where(low, 0.0, qf)
        k_s[r, rows:2 * rows, :] = k_ref[:, cols].astype(F32)
        v_s[r, rows:2 * rows, :] = v_ref[:, cols].astype(F32)

    ones = jnp.ones((2 * rows, LANES), BF16)
    pick = (lax.broadcasted_iota(jnp.int32, (2 * rows, LANES), 0) % rows
            == lax.broadcasted_iota(jnp.int32, (2 * rows, LANES), 1))
    pick = jnp.where(pick, 1.0, 0.0).astype(BF16)
    nt = (((1,), (1,)), ((), ()))

    def attend(q0_pieces, q1_pieces, k_pieces, v_pieces, bias):
        q2 = jnp.concatenate(q0_pieces + q1_pieces, axis=0).astype(BF16)
        k2 = jnp.concatenate(k_pieces, axis=0).astype(BF16)
        v2 = jnp.concatenate(v_pieces, axis=0).astype(BF16)
        s = lax.dot_general(jnp.concatenate([q2, pick], axis=1), jnp.concatenate([k2, bias], axis=1), nt,
                            preferred_element_type=F32)
        m = jnp.max(s, axis=-1, keepdims=True)
        p = jnp.exp2(s - m).astype(BF16)
        pv = jnp.dot(p, jnp.concatenate([v2, ones], axis=1), preferred_element_type=F32)
        return (jnp.where(low, pv[:rows, :LANES], pv[rows:, :LANES]),
                jnp.where(low, pv[:rows, LANES:], pv[rows:, LANES:]),
                jnp.where(low, m[:rows], m[rows:]))

    def variant(cond):
        return jnp.where(cond, 1, 0)

    def store(slot, pieces, results):
        for res, rows_dst, rows_src in pieces:
            for ref, val in zip((num_s, den_s, max_s), results):
                ref[slot, res, rows_dst, :] = val[rows_src]

    for c in range(4):
        for j in range(4):
            qrow, krow = slice(32 * j, 32 * j + 32), slice(96 + 32 * j, 160 + 32 * j)
            res = attend([q0_s[4 * a + c, qrow, :] for a in range(4)],
                         [q1_s[4 * a + c, qrow, :] for a in range(4)],
                         [k_s[4 * a + c, krow, :] for a in range(4)],
                         [v_s[4 * a + c, krow, :] for a in range(4)],
                         bias_ref[2 + variant(first_tile)] if j == 0 else bias_ref[2])
            store(0, [(4 * a + c, qrow, slice(32 * a, 32 * a + 32)) for a in range(4)], res)

    for j in range(MAX_DIL):
        qrow, krow = slice(8 * j, 8 * j + 8), slice(120 + 8 * j, 136 + 8 * j)
        res = attend([q0_s[r, qrow, :] for r in range(MAX_DIL)],
                     [q1_s[r, qrow, :] for r in range(MAX_DIL)],
                     [k_s[r, krow, :] for r in range(MAX_DIL)],
                     [v_s[r, krow, :] for r in range(MAX_DIL)],
                     bias_ref[4 + variant(first_tile)] if j == 0 else bias_ref[4])
        store(1, [(r, qrow, slice(8 * r, 8 * r + 8)) for r in range(MAX_DIL)], res)

    for r in range(MAX_DIL):
        n0, d0, m0 = attend([q0_s[r]], [q1_s[r]], [k_s[r]], [v_s[r]], bias_ref[variant(first_tile)])
        m1, m2 = max_s[0, r], max_s[1, r]
        top = jnp.maximum(jnp.maximum(m0, m1), m2)
        e0, e1, e2 = jnp.exp2(m0 - top), jnp.exp2(m1 - top), jnp.exp2(m2 - top)
        num = e0 * n0 + e1 * num_s[0, r] + e2 * num_s[1, r]
        den = e0 * d0 + e1 * den_s[0, r] + e2 * den_s[1, r]
        o_ref[:, r * LANES:(r + 1) * LANES] = (num * (1.0 / den)).astype(BF16)
        k_s[r, 0:rows, :] = k_s[r, rows:2 * rows, :]
        v_s[r, 0:rows, :] = v_s[r, rows:2 * rows, :]


def _dilated_attention(qa, ka, va):
    b, n_pairs, view_len, width = qa.shape
    tile = pl.BlockSpec((None, ATTN_PAIRS, ATTN_ROWS, width), lambda bi, hp, i: (bi, hp, i, 0))
    biases = _attention_biases()
    scratch = lambda n: pltpu.VMEM((ATTN_PAIRS, MAX_DIL, n, LANES), F32)
    return pl.pallas_call(
        _attention_kernel,
        grid=(b, n_pairs // ATTN_PAIRS, view_len // ATTN_ROWS),
        in_specs=[pl.BlockSpec(biases.shape, lambda bi, hp, i: (0, 0, 0)), tile, tile, tile],
        out_specs=tile,
        out_shape=jax.ShapeDtypeStruct(qa.shape, BF16),
        scratch_shapes=[scratch(ATTN_ROWS), scratch(ATTN_ROWS), scratch(2 * ATTN_ROWS), scratch(2 * ATTN_ROWS)]
        + [pltpu.VMEM((ATTN_PAIRS, len(DILATIONS) - 1, MAX_DIL, ATTN_ROWS, LANES), F32)] * 3,
        compiler_params=pltpu.CompilerParams(dimension_semantics=("arbitrary", "arbitrary", "arbitrary"),
                                             vmem_limit_bytes=VMEM_LIMIT_BYTES),
        name="dilated_attention",
    )(biases, qa, ka, va)


def _outproj_ffn_kernel(x_ref, attn_ref, ret_ref, gpost_ref, gpre_ref, gffn_ref, wout_ref, wup_ref, cw_ref, cb_ref,
                        wdown_ref, o_ref, act_s, halo_s, stage_s):
    tm = x_ref.shape[0]
    d_ff = wdown_ref.shape[0]

    @pl.when(pl.program_id(1) == 0)
    def _():
        halo_s[...] = jnp.zeros_like(halo_s)

    first_rows = lax.broadcasted_iota(jnp.int32, (SUBLANES, 1), 0)
    hm = tm // FF_STREAMS
    vm = hm // MAX_DIL
    x1, hn = [], []
    for st in range(FF_STREAMS):
        rows = slice(st * hm, (st + 1) * hm)
        for hp in range(N_PAIRS):
            for r in range(MAX_DIL):
                piece = attn_ref[hp, st * vm:(st + 1) * vm, r * LANES:(r + 1) * LANES].astype(F32)
                stage_s[hp, pl.ds(st * hm + r, vm, stride=MAX_DIL), :] = piece
        mixed = jnp.concatenate([stage_s[hp, rows, :].astype(BF16) for hp in range(N_PAIRS)] + [ret_ref[rows, :]],
                                axis=-1)
        mix = jnp.dot(mixed, wout_ref[...], preferred_element_type=F32)
        x1.append(x_ref[rows, :] + _rms(mix, gpost_ref[...]))
        hn.append(_rms(x1[st], gpre_ref[...]).astype(BF16))

    def conv(cols, u):
        halo = halo_s[:, cols]
        halo_s[:, cols] = u[hm - SUBLANES:hm]
        taps = []
        for shift in (1, 2):
            rolled = pltpu.roll(u, shift, axis=0)
            head = jnp.where(first_rows < shift, pltpu.roll(halo, shift, axis=0), rolled[0:SUBLANES])
            taps.append(jnp.concatenate([head, rolled[SUBLANES:]], axis=0))
        return cb_ref[:, cols] + cw_ref[2:3, cols] * u + cw_ref[1:2, cols] * taps[0] + cw_ref[0:1, cols] * taps[1]

    for c in range(d_ff // FF_CHUNK):
        gcols = slice(c * FF_CHUNK, (c + 1) * FF_CHUNK)
        ucols = slice(d_ff + c * FF_CHUNK, d_ff + (c + 1) * FF_CHUNK)
        for st in range(FF_STREAMS):
            gate = conv(gcols, jnp.dot(hn[st], wup_ref[:, gcols], preferred_element_type=F32))
            half_up = conv(ucols, jnp.dot(hn[st], wup_ref[:, ucols], preferred_element_type=F32))
            inner = gate * (GELU_C1 + GELU_C3 * (gate * gate))
            act_s[st * hm:(st + 1) * hm, gcols] = (gate * (1.0 + jnp.tanh(inner)) * half_up).astype(BF16)

    for st in range(FF_STREAMS):
        rows = slice(st * hm, (st + 1) * hm)
        ffn = jnp.dot(act_s[rows, :], wdown_ref[...], preferred_element_type=F32)
        o_ref[rows, :] = x1[st] + _rms(ffn, gffn_ref[...])


def _outproj_ffn(x, attn, ret, g_post, g_pre, g_ffn, w_out_bf, w_up_bf, conv_w, conv_b, w_down_bf):
    b, s, d = x.shape
    tm = TOKEN_TILE
    d_ff = w_down_bf.shape[0]
    const = lambda shape: pl.BlockSpec(shape, lambda bi, i: (0,) * len(shape), pipeline_mode=pl.Buffered(1))
    return pl.pallas_call(
        _outproj_ffn_kernel,
        grid=(b, s // tm),
        in_specs=[
            pl.BlockSpec((None, tm, d), lambda bi, i: (bi, i, 0)),
            pl.BlockSpec((None, N_PAIRS, tm // MAX_DIL, MAX_DIL * LANES), lambda bi, i: (bi, 0, i, 0)),
            pl.BlockSpec((None, tm, RET_V_WIDTH), lambda bi, i: (bi, i, 0)),
            const((1, d)), const((1, d)), const((1, d)),
            const(w_out_bf.shape), const(w_up_bf.shape), const(conv_w.shape), const(conv_b.shape),
            const(w_down_bf.shape),
        ],
        out_specs=pl.BlockSpec((None, tm, d), lambda bi, i: (bi, i, 0)),
        out_shape=jax.ShapeDtypeStruct((b, s, d), F32),
        scratch_shapes=[pltpu.VMEM((tm, d_ff), BF16),
                        pltpu.VMEM((SUBLANES, 2 * d_ff), F32),
                        pltpu.VMEM((N_PAIRS, tm, LANES), F32)],
        compiler_params=pltpu.CompilerParams(dimension_semantics=("arbitrary", "arbitrary"),
                                             vmem_limit_bytes=VMEM_LIMIT_BYTES),
        name="outproj_ffn",
    )(x, attn, ret, g_post, g_pre, g_ffn, w_out_bf, w_up_bf, conv_w, conv_b, w_down_bf)


def _layer(x, g_mix_pre, g_mix_post, w_in, w_out, g_ffn_pre, g_ffn_post, w_up, conv_w, conv_b, w_down):
    d = x.shape[-1]
    d_ff = w_down.shape[0]
    col_scale = np.ones((w_in.shape[1],), np.float32)
    col_scale[:ATTN_WIDTH] = ATTN_HEAD_DIM ** -0.5 * np.log2(np.e)
    kr0 = 3 * ATTN_WIDTH + RET_QK_WIDTH
    col_scale[kr0:kr0 + RET_QK_WIDTH] = RET_QK_DIM ** -0.5
    w_in_bf = (w_in * col_scale).astype(BF16)
    qa, ka, va, ret = _inproj_retention(x, g_mix_pre.reshape(1, d), w_in_bf)
    attn = _dilated_attention(qa, ka, va)
    up_half = np.concatenate([np.ones((d_ff,), np.float32), np.full((d_ff,), 0.5, np.float32)])
    return _outproj_ffn(x, attn, ret, g_mix_post.reshape(1, d), g_ffn_pre.reshape(1, d), g_ffn_post.reshape(1, d),
                        w_out.astype(BF16), w_up.astype(BF16), conv_w * up_half, conv_b.reshape(1, -1) * up_half,
                        w_down.astype(BF16))


def kernel(x, mix_pre_gain, mix_post_gain, w_in, w_out, ffn_pre_gain, ffn_post_gain, w_up, conv_w, conv_b, w_down):
    for layer in range(mix_pre_gain.shape[0]):
        x = _layer(x, mix_pre_gain[layer], mix_post_gain[layer], w_in[layer], w_out[layer], ffn_pre_gain[layer],
                   ffn_post_gain[layer], w_up[layer], conv_w[layer], conv_b[layer], w_down[layer])
    return x
```

```python
import numpy as np
import jax
import jax.numpy as jnp
from jax import lax
from jax.experimental import pallas as pl
from jax.experimental.pallas import tpu as pltpu

F32 = jnp.float32
BF16 = jnp.bfloat16

ATTN_HEAD_DIM = 64
N_ATTN_HEADS = 8
ATTN_WIDTH = N_ATTN_HEADS * ATTN_HEAD_DIM
ATTN_STEPS = 128
DILATIONS = (1, 4, 16)
RET_V_DIM = 128
N_RET_HEADS = 4
RET_QK_DIM = 64
RET_QK_WIDTH = N_RET_HEADS * RET_QK_DIM
RET_V_WIDTH = N_RET_HEADS * RET_V_DIM
ROPE_BASE = 10000.0
CONV_WIDTH = 3
NORM_EPS = 1e-6
MASK_VALUE = -1e30

LANES = 128
SUBLANES = 8
HEAD_PAIR = LANES // ATTN_HEAD_DIM
N_PAIRS = N_ATTN_HEADS // HEAD_PAIR
MAX_DIL = max(DILATIONS)
ATTN_ROWS = 128
TOKEN_TILE = 512
INPROJ_TILE = 1024
RET_CHUNK = 256
ATTN_PAIRS = 2
FF_CHUNK = 256
FF_STREAMS = 2
GELU_C1 = float(np.sqrt(2.0 / np.pi).astype(np.float32))
GELU_C3 = float(np.float32(GELU_C1 * 0.044715))
VMEM_LIMIT_BYTES = 56 * 1024 * 1024


def _pad_columns(w):
    return jnp.pad(w, ((0, 0), (0, LANES)))


def _dot_resident(lhs, w_ref):
    return jnp.dot(lhs, w_ref[:, :w_ref.shape[1] - LANES], preferred_element_type=F32)


def _rms(x, gain):
    return x * lax.rsqrt(jnp.mean(x * x, axis=-1, keepdims=True) + NORM_EPS) * gain


def _inproj_retention_kernel(x_ref, gain_ref, w_ref, cos_ref, sin_ref, dmat_ref, qdec_ref, kdec_ref, sdec_ref,
                             qa_ref, ka_ref, va_ref, ret_ref, state_ref, stage_ref):
    @pl.when(pl.program_id(1) == 0)
    def _():
        state_ref[...] = jnp.zeros_like(state_ref)

    base = 3 * ATTN_WIDTH
    vbase = base + 2 * RET_QK_WIDTH
    gbase = vbase + RET_V_WIDTH
    lane = lax.broadcasted_iota(jnp.int32, (1, RET_QK_WIDTH), 1)
    first_half = (lane % RET_QK_DIM) < (RET_QK_DIM // 2)
    lane128 = lax.broadcasted_iota(jnp.int32, (1, LANES), 1)
    low_lanes = lane128 < RET_QK_DIM
    row128 = lax.broadcasted_iota(jnp.int32, (LANES, 1), 0)
    low_rows = row128 < RET_QK_DIM
    nt = (((1,), (1,)), ((), ()))
    tn = (((0,), (0,)), ((), ()))
    view_rows = RET_CHUNK // MAX_DIL
    pitch = stage_ref.shape[1] // MAX_DIL

    for c in range(x_ref.shape[0] // RET_CHUNK):
        rows = slice(c * RET_CHUNK, (c + 1) * RET_CHUNK)
        h = _rms(x_ref[rows, :], gain_ref[...]).astype(BF16)
        proj = _dot_resident(h, w_ref)

        for t, out_ref in enumerate((qa_ref, ka_ref, va_ref)):
            for hp in range(N_PAIRS):
                col = t * N_PAIRS + hp
                block = proj[:, col * LANES:(col + 1) * LANES]
                for g in range(RET_CHUNK // SUBLANES):
                    m = c * view_rows + g * SUBLANES // MAX_DIL
                    r0 = g * SUBLANES % MAX_DIL
                    stage_ref[col, pl.ds(r0 * pitch + m, SUBLANES, stride=pitch), :] = (
                        block[g * SUBLANES:(g + 1) * SUBLANES])
                for r in range(MAX_DIL):
                    piece = stage_ref[col, r * pitch + c * view_rows:r * pitch + (c + 1) * view_rows, :]
                    out_ref[hp, c * view_rows:(c + 1) * view_rows, r * LANES:(r + 1) * LANES] = piece.astype(BF16)

        def rotary(t):
            swapped = jnp.where(first_half,
                                pltpu.roll(t, RET_QK_WIDTH - RET_QK_DIM // 2, axis=1),
                                pltpu.roll(t, RET_QK_DIM // 2, axis=1))
            return t * cos_ref[rows, :] + swapped * sin_ref[rows, :]

        qr = rotary(proj[:, base:base + RET_QK_WIDTH])
        kr = rotary(proj[:, base + RET_QK_WIDTH:base + 2 * RET_QK_WIDTH])

        for p in range(N_RET_HEADS // 2):
            cols = slice(p * LANES, (p + 1) * LANES)
            q2 = qr[:, cols]
            k2 = kr[:, cols]
            q_heads = (jnp.where(low_lanes, q2, 0.0), jnp.where(low_lanes, 0.0, q2))
            lhs = jnp.concatenate(q_heads, axis=0).astype(BF16)
            scores = lax.dot_general(lhs, k2.astype(BF16), nt, preferred_element_type=F32)
            kdec = (k2 * kdec_ref[:, cols]).astype(BF16)
            state = state_ref[p]
            state_bf = state.astype(BF16)
            kv = []
            for hh in range(2):
                head = 2 * p + hh
                v_h = proj[:, vbase + head * RET_V_DIM:vbase + (head + 1) * RET_V_DIM].astype(BF16)
                s_h = (scores[hh * RET_CHUNK:(hh + 1) * RET_CHUNK] * dmat_ref[head]).astype(BF16)
                inner = jnp.dot(s_h, v_h, preferred_element_type=F32)
                q_dec = (q_heads[hh] * qdec_ref[:, cols]).astype(BF16)
                cross = jnp.dot(q_dec, state_bf, preferred_element_type=F32)
                o = inner + cross
                o = o * lax.rsqrt(jnp.mean(o * o, axis=-1, keepdims=True) + NORM_EPS)
                g = proj[:, gbase + head * RET_V_DIM:gbase + (head + 1) * RET_V_DIM]
                o = o * (g * (1.0 / (1.0 + jnp.exp(-g))))
                ret_ref[rows, head * RET_V_DIM:(head + 1) * RET_V_DIM] = o.astype(BF16)
                kv.append(lax.dot_general(kdec, v_h, tn, preferred_element_type=F32))
            state_ref[p] = sdec_ref[p] * state + jnp.where(low_rows, kv[0], kv[1])


def _retention_constants(seq):
    freqs = ROPE_BASE ** (-np.arange(0, RET_QK_DIM, 2, dtype=np.float64) / RET_QK_DIM)
    ang = np.arange(seq, dtype=np.float64)[:, None] * freqs[None]
    cos_h = np.concatenate([np.cos(ang), np.cos(ang)], axis=1)
    sin_h = np.concatenate([-np.sin(ang), np.sin(ang)], axis=1)
    cos_t = np.tile(cos_h, (1, N_RET_HEADS)).astype(np.float32)
    sin_t = np.tile(sin_h, (1, N_RET_HEADS)).astype(np.float32)
    log_gamma = np.log1p(-np.exp2(-5.0 - np.arange(N_RET_HEADS, dtype=np.float64)))
    idx = np.arange(RET_CHUNK, dtype=np.float64)
    rel = idx[:, None] - idx[None, :]
    dmat = np.where(rel >= 0, np.exp(log_gamma[:, None, None] * np.maximum(rel, 0.0)), 0.0)
    qdec = np.exp(log_gamma[None, :] * (idx + 1.0)[:, None])
    kdec = np.exp(log_gamma[None, :] * (RET_CHUNK - 1 - idx)[:, None])
    qdec = np.repeat(qdec, RET_QK_DIM, axis=1)
    kdec = np.repeat(kdec, RET_QK_DIM, axis=1)
    sdec = np.repeat(np.exp(log_gamma * RET_CHUNK), RET_QK_DIM).reshape(N_RET_HEADS // 2, LANES, 1)
    sdec = np.broadcast_to(sdec, (N_RET_HEADS // 2, LANES, LANES))
    f = lambda a: jnp.asarray(np.ascontiguousarray(a), dtype=F32)
    return f(cos_t), f(sin_t), f(dmat), f(qdec), f(kdec), f(sdec)


def _inproj_retention(x, gain, w_in_bf):
    b, s, d = x.shape
    tm = INPROJ_TILE
    in_width = w_in_bf.shape[1]
    cos_t, sin_t, dmat, qdec, kdec, sdec = _retention_constants(s)
    const = lambda shape: pl.BlockSpec(shape, lambda bi, i: (0,) * len(shape))
    pair_spec = pl.BlockSpec((None, N_PAIRS, tm // MAX_DIL, MAX_DIL * LANES), lambda bi, i: (bi, 0, i, 0))
    pair_shape = jax.ShapeDtypeStruct((b, N_PAIRS, s // MAX_DIL, MAX_DIL * LANES), BF16)
    return pl.pallas_call(
        _inproj_retention_kernel,
        grid=(b, s // tm),
        in_specs=[
            pl.BlockSpec((None, tm, d), lambda bi, i: (bi, i, 0)),
            const((1, d)),
            const((d, in_width)),
            pl.BlockSpec((tm, RET_QK_WIDTH), lambda bi, i: (i, 0)),
            pl.BlockSpec((tm, RET_QK_WIDTH), lambda bi, i: (i, 0)),
            const((N_RET_HEADS, RET_CHUNK, RET_CHUNK)),
            const((RET_CHUNK, RET_QK_WIDTH)),
            const((RET_CHUNK, RET_QK_WIDTH)),
            const((N_RET_HEADS // 2, LANES, LANES)),
        ],
        out_specs=[pair_spec, pair_spec, pair_spec,
                   pl.BlockSpec((None, tm, RET_V_WIDTH), lambda bi, i: (bi, i, 0))],
        out_shape=[pair_shape, pair_shape, pair_shape, jax.ShapeDtypeStruct((b, s, RET_V_WIDTH), BF16)],
        scratch_shapes=[pltpu.VMEM((N_RET_HEADS // 2, LANES, LANES), F32),
                        pltpu.VMEM((3 * N_PAIRS, MAX_DIL * (tm // MAX_DIL + SUBLANES), LANES), F32)],
        compiler_params=pltpu.CompilerParams(dimension_semantics=("arbitrary", "arbitrary"),
                                             vmem_limit_bytes=VMEM_LIMIT_BYTES),
        name="inproj_retention",
    )(x, gain, w_in_bf, cos_t, sin_t, dmat, qdec, kdec, sdec)


def _attention_biases():
    q = np.arange(ATTN_ROWS)[None, :]
    k = np.arange(2 * ATTN_ROWS)[:, None]
    out = []
    out += [(ATTN_ROWS + q - k, k >= ATTN_ROWS)]
    out += [(4 * (32 + q % 32 - k % 64) + q // 32 - k // 64, k % 64 >= 32)]
    out += [(16 * (8 + q % 8 - k % 16) + q // 8 - k // 16, k % 16 >= 8)]
    biases = []
    for dist, in_seq in out:
        band = (dist >= 0) & (dist <= ATTN_STEPS)
        biases.append(np.where(band, 0.0, MASK_VALUE))
        biases.append(np.where(band & in_seq, 0.0, MASK_VALUE))
    return jnp.asarray(np.stack(biases), dtype=BF16)


def _attention_kernel(bias_ref, q_ref, k_ref, v_ref, o_ref, *scratch):
    first_tile = pl.program_id(2) == 0
    for hs in range(ATTN_PAIRS):
        _attention_pair(first_tile, bias_ref, q_ref.at[hs], k_ref.at[hs], v_ref.at[hs], o_ref.at[hs],
                        *(ref.at[hs] for ref in scratch))


def _attention_pair(first_tile, bias_ref, q_ref, k_ref, v_ref, o_ref, q0_s, q1_s, k_s, v_s, num_s, den_s, max_s):
    rows = ATTN_ROWS

    @pl.when(first_tile)
    def _():
        k_s[:, 0:rows, :] = jnp.zeros((MAX_DIL, rows, LANES), F32)
        v_s[:, 0:rows, :] = jnp.zeros((MAX_DIL, rows, LANES), F32)

    lane = lax.broadcasted_iota(jnp.int32, (1, LANES), 1)
    low = lane < ATTN_HEAD_DIM
    for r in range(MAX_DIL):
        cols = slice(r * LANES, (r + 1) * LANES)
        qf = q_ref[:, cols].astype(F32)
        q0_s[r] = jnp.where(low, qf, 0.0)
        q1_s[r] = jnp.where(low, 0.0, qf)
        k_s[r, rows:2 * rows, :] = k_ref[:, cols].astype(F32)
        v_s[r, rows:2 * rows, :] = v_ref[:, cols].astype(F32)

    ones = jnp.ones((2 * rows, LANES), BF16)
    pick = (lax.broadcasted_iota(jnp.int32, (2 * rows, LANES), 0) % rows
            == lax.broadcasted_iota(jnp.int32, (2 * rows, LANES), 1))
    pick = jnp.where(pick, 1.0, 0.0).astype(BF16)
    nt = (((1,), (1,)), ((), ()))

    def attend(q0_pieces, q1_pieces, k_pieces, v_pieces, bias):
        q2 = jnp.concatenate(q0_pieces + q1_pieces, axis=0).astype(BF16)
        k2 = jnp.concatenate(k_pieces, axis=0).astype(BF16)
        v2 = jnp.concatenate(v_pieces, axis=0).astype(BF16)
        s = lax.dot_general(jnp.concatenate([q2, pick], axis=1), jnp.concatenate([k2, bias], axis=1), nt,
                            preferred_element_type=F32)
        m = jnp.max(s, axis=-1, keepdims=True)
        p = jnp.exp2(s - m).astype(BF16)
        pv = jnp.dot(p, jnp.concatenate([v2, ones], axis=1), preferred_element_type=F32)
        return (jnp.where(low, pv[:rows, :LANES], pv[rows:, :LANES]),
                jnp.where(low, pv[:rows, LANES:], pv[rows:, LANES:]),
                jnp.where(low, m[:rows], m[rows:]))

    def variant(cond):
        return jnp.where(cond, 1, 0)

    def store(slot, pieces, results):
        for res, rows_dst, rows_src in pieces:
            for ref, val in zip((num_s, den_s, max_s), results):
                ref[slot, res, rows_dst, :] = val[rows_src]

    for c in range(4):
        for j in range(4):
            qrow, krow = slice(32 * j, 32 * j + 32), slice(96 + 32 * j, 160 + 32 * j)
            res = attend([q0_s[4 * a + c, qrow, :] for a in range(4)],
                         [q1_s[4 * a + c, qrow, :] for a in range(4)],
                         [k_s[4 * a + c, krow, :] for a in range(4)],
                         [v_s[4 * a + c, krow, :] for a in range(4)],
                         bias_ref[2 + variant(first_tile)] if j == 0 else bias_ref[2])
            store(0, [(4 * a + c, qrow, slice(32 * a, 32 * a + 32)) for a in range(4)], res)

    for j in range(MAX_DIL):
        qrow, krow = slice(8 * j, 8 * j + 8), slice(120 + 8 * j, 136 + 8 * j)
        res = attend([q0_s[r, qrow, :] for r in range(MAX_DIL)],
                     [q1_s[r, qrow, :] for r in range(MAX_DIL)],
                     [k_s[r, krow, :] for r in range(MAX_DIL)],
                     [v_s[r, krow, :] for r in range(MAX_DIL)],
                     bias_ref[4 + variant(first_tile)] if j == 0 else bias_ref[4])
        store(1, [(r, qrow, slice(8 * r, 8 * r + 8)) for r in range(MAX_DIL)], res)

    for r in range(MAX_DIL):
        n0, d0, m0 = attend([q0_s[r]], [q1_s[r]], [k_s[r]], [v_s[r]], bias_ref[variant(first_tile)])
        m1, m2 = max_s[0, r], max_s[1, r]
        top = jnp.maximum(jnp.maximum(m0, m1), m2)
        e0, e1, e2 = jnp.exp2(m0 - top), jnp.exp2(m1 - top), jnp.exp2(m2 - top)
        num = e0 * n0 + e1 * num_s[0, r] + e2 * num_s[1, r]
        den = e0 * d0 + e1 * den_s[0, r] + e2 * den_s[1, r]
        o_ref[:, r * LANES:(r + 1) * LANES] = (num * (1.0 / den)).astype(BF16)
        k_s[r, 0:rows, :] = k_s[r, rows:2 * rows, :]
        v_s[r, 0:rows, :] = v_s[r, rows:2 * rows, :]


def _dilated_attention(qa, ka, va):
    b, n_pairs, view_len, width = qa.shape
    tile = pl.BlockSpec((None, ATTN_PAIRS, ATTN_ROWS, width), lambda bi, hp, i: (bi, hp, i, 0))
    biases = _attention_biases()
    scratch = lambda n: pltpu.VMEM((ATTN_PAIRS, MAX_DIL, n, LANES), F32)
    return pl.pallas_call(
        _attention_kernel,
        grid=(b, n_pairs // ATTN_PAIRS, view_len // ATTN_ROWS),
        in_specs=[pl.BlockSpec(biases.shape, lambda bi, hp, i: (0, 0, 0)), tile, tile, tile],
        out_specs=tile,
        out_shape=jax.ShapeDtypeStruct(qa.shape, BF16),
        scratch_shapes=[scratch(ATTN_ROWS), scratch(ATTN_ROWS), scratch(2 * ATTN_ROWS), scratch(2 * ATTN_ROWS)]
        + [pltpu.VMEM((ATTN_PAIRS, len(DILATIONS) - 1, MAX_DIL, ATTN_ROWS, LANES), F32)] * 3,
        compiler_params=pltpu.CompilerParams(dimension_semantics=("arbitrary", "arbitrary", "arbitrary"),
                                             vmem_limit_bytes=VMEM_LIMIT_BYTES),
        name="dilated_attention",
    )(biases, qa, ka, va)


def _outproj_ffn_kernel(x_ref, attn_ref, ret_ref, gpost_ref, gpre_ref, gffn_ref, wout_ref, wup_ref, cw_ref, cb_ref,
                        wdown_ref, o_ref, act_s, halo_s, stage_s):
    tm = x_ref.shape[0]
    d_ff = wdown_ref.shape[0]

    @pl.when(pl.program_id(1) == 0)
    def _():
        halo_s[...] = jnp.zeros_like(halo_s)

    first_rows = lax.broadcasted_iota(jnp.int32, (SUBLANES, 1), 0)
    hm = tm // FF_STREAMS
    vm = hm // MAX_DIL
    x1, hn = [], []
    for st in range(FF_STREAMS):
        rows = slice(st * hm, (st + 1) * hm)
        for hp in range(N_PAIRS):
            for r in range(MAX_DIL):
                piece = attn_ref[hp, st * vm:(st + 1) * vm, r * LANES:(r + 1) * LANES].astype(F32)
                stage_s[hp, pl.ds(st * hm + r, vm, stride=MAX_DIL), :] = piece
        mixed = jnp.concatenate([stage_s[hp, rows, :].astype(BF16) for hp in range(N_PAIRS)] + [ret_ref[rows, :]],
                                axis=-1)
        mix = _dot_resident(mixed, wout_ref)
        x1.append(x_ref[rows, :] + _rms(mix, gpost_ref[...]))
        hn.append(_rms(x1[st], gpre_ref[...]).astype(BF16))

    def conv(cols, u):
        halo = halo_s[:, cols]
        halo_s[:, cols] = u[hm - SUBLANES:hm]
        taps = []
        for shift in (1, 2):
            rolled = pltpu.roll(u, shift, axis=0)
            head = jnp.where(first_rows < shift, pltpu.roll(halo, shift, axis=0), rolled[0:SUBLANES])
            taps.append(jnp.concatenate([head, rolled[SUBLANES:]], axis=0))
        return cb_ref[:, cols] + cw_ref[2:3, cols] * u + cw_ref[1:2, cols] * taps[0] + cw_ref[0:1, cols] * taps[1]

    for c in range(d_ff // FF_CHUNK):
        gcols = slice(c * FF_CHUNK, (c + 1) * FF_CHUNK)
        ucols = slice(d_ff + c * FF_CHUNK, d_ff + (c + 1) * FF_CHUNK)
        for st in range(FF_STREAMS):
            gate = conv(gcols, jnp.dot(hn[st], wup_ref[:, gcols], preferred_element_type=F32))
            half_up = conv(ucols, jnp.dot(hn[st], wup_ref[:, ucols], preferred_element_type=F32))
            inner = gate * (GELU_C1 + GELU_C3 * (gate * gate))
            act_s[st * hm:(st + 1) * hm, gcols] = (gate * (1.0 + jnp.tanh(inner)) * half_up).astype(BF16)

    for st in range(FF_STREAMS):
        rows = slice(st * hm, (st + 1) * hm)
        ffn = _dot_resident(act_s[rows, :], wdown_ref)
        o_ref[rows, :] = x1[st] + _rms(ffn, gffn_ref[...])


def _outproj_ffn(x, attn, ret, g_post, g_pre, g_ffn, w_out_bf, w_up_bf, conv_w, conv_b, w_down_bf):
    b, s, d = x.shape
    tm = TOKEN_TILE
    d_ff = w_down_bf.shape[0]
    const = lambda shape: pl.BlockSpec(shape, lambda bi, i: (0,) * len(shape), pipeline_mode=pl.Buffered(1))
    return pl.pallas_call(
        _outproj_ffn_kernel,
        grid=(b, s // tm),
        in_specs=[
            pl.BlockSpec((None, tm, d), lambda bi, i: (bi, i, 0)),
            pl.BlockSpec((None, N_PAIRS, tm // MAX_DIL, MAX_DIL * LANES), lambda bi, i: (bi, 0, i, 0)),
            pl.BlockSpec((None, tm, RET_V_WIDTH), lambda bi, i: (bi, i, 0)),
            const((1, d)), const((1, d)), const((1, d)),
            const(w_out_bf.shape), const(w_up_bf.shape), const(conv_w.shape), const(conv_b.shape),
            const(w_down_bf.shape),
        ],
        out_specs=pl.BlockSpec((None, tm, d), lambda bi, i: (bi, i, 0)),
        out_shape=jax.ShapeDtypeStruct((b, s, d), F32),
        scratch_shapes=[pltpu.VMEM((tm, d_ff), BF16),
                        pltpu.VMEM((SUBLANES, 2 * d_ff), F32),
                        pltpu.VMEM((N_PAIRS, tm, LANES), F32)],
        compiler_params=pltpu.CompilerParams(dimension_semantics=("arbitrary", "arbitrary"),
                                             vmem_limit_bytes=VMEM_LIMIT_BYTES),
        name="outproj_ffn",
    )(x, attn, ret, g_post, g_pre, g_ffn, w_out_bf, w_up_bf, conv_w, conv_b, w_down_bf)


def _layer(x, g_mix_pre, g_mix_post, w_in, w_out, g_ffn_pre, g_ffn_post, w_up, conv_w, conv_b, w_down):
    d = x.shape[-1]
    d_ff = w_down.shape[0]
    col_scale = np.ones((w_in.shape[1],), np.float32)
    col_scale[:ATTN_WIDTH] = ATTN_HEAD_DIM ** -0.5 * np.log2(np.e)
    kr0 = 3 * ATTN_WIDTH + RET_QK_WIDTH
    col_scale[kr0:kr0 + RET_QK_WIDTH] = RET_QK_DIM ** -0.5
    w_in_bf = _pad_columns((w_in * col_scale).astype(BF16))
    qa, ka, va, ret = _inproj_retention(x, g_mix_pre.reshape(1, d), w_in_bf)
    attn = _dilated_attention(qa, ka, va)
    up_half = np.concatenate([np.ones((d_ff,), np.float32), np.full((d_ff,), 0.5, np.float32)])
    return _outproj_ffn(x, attn, ret, g_mix_post.reshape(1, d), g_ffn_pre.reshape(1, d), g_ffn_post.reshape(1, d),
                        _pad_columns(w_out.astype(BF16)), _pad_columns(w_up.astype(BF16)), conv_w * up_half,
                        conv_b.reshape(1, -1) * up_half, _pad_columns(w_down.astype(BF16)))


def kernel(x, mix_pre_gain, mix_post_gain, w_in, w_out, ffn_pre_gain, ffn_post_gain, w_up, conv_w, conv_b, w_down):
    for layer in range(mix_pre_gain.shape[0]):
        x = _layer(x, mix_pre_gain[layer], mix_post_gain[layer], w_in[layer], w_out[layer], ffn_pre_gain[layer],
                   ffn_post_gain[layer], w_up[layer], conv_w[layer], conv_b[layer], w_down[layer])
    return x
```

```python
import numpy as np
import jax
import jax.numpy as jnp
from jax import lax
from jax.experimental import pallas as pl
from jax.experimental.pallas import tpu as pltpu

F32 = jnp.float32
BF16 = jnp.bfloat16

ATTN_HEAD_DIM = 64
N_ATTN_HEADS = 8
ATTN_WIDTH = N_ATTN_HEADS * ATTN_HEAD_DIM
ATTN_STEPS = 128
DILATIONS = (1, 4, 16)
RET_V_DIM = 128
N_RET_HEADS = 4
RET_QK_DIM = 64
RET_QK_WIDTH = N_RET_HEADS * RET_QK_DIM
RET_V_WIDTH = N_RET_HEADS * RET_V_DIM
ROPE_BASE = 10000.0
CONV_WIDTH = 3
NORM_EPS = 1e-6
MASK_VALUE = -1e30

LANES = 128
SUBLANES = 8
HEAD_PAIR = LANES // ATTN_HEAD_DIM
N_PAIRS = N_ATTN_HEADS // HEAD_PAIR
MAX_DIL = max(DILATIONS)
ATTN_ROWS = 128
TOKEN_TILE = 512
INPROJ_TILE = 1024
RET_CHUNK = 256
ATTN_PAIRS = 2
FF_CHUNK = 256
FF_STREAMS = 2
GELU_C1 = float(np.sqrt(2.0 / np.pi).astype(np.float32))
GELU_C3 = float(np.float32(GELU_C1 * 0.044715))
VMEM_LIMIT_BYTES = 56 * 1024 * 1024


def _pad_columns(w):
    return jnp.concatenate([w, jnp.zeros((w.shape[0], LANES), w.dtype)], axis=1)


def _dot_resident(lhs, w_ref):
    return jnp.dot(lhs, w_ref[:, :w_ref.shape[1] - LANES], preferred_element_type=F32)


def _rms(x, gain):
    return x * lax.rsqrt(jnp.mean(x * x, axis=-1, keepdims=True) + NORM_EPS) * gain


def _inproj_retention_kernel(x_ref, gain_ref, w_ref, cos_ref, sin_ref, dmat_ref, qdec_ref, kdec_ref, sdec_ref,
                             qa_ref, ka_ref, va_ref, ret_ref, state_ref, stage_ref):
    @pl.when(pl.program_id(1) == 0)
    def _():
        state_ref[...] = jnp.zeros_like(state_ref)

    base = 3 * ATTN_WIDTH
    vbase = base + 2 * RET_QK_WIDTH
    gbase = vbase + RET_V_WIDTH
    lane = lax.broadcasted_iota(jnp.int32, (1, RET_QK_WIDTH), 1)
    first_half = (lane % RET_QK_DIM) < (RET_QK_DIM // 2)
    lane128 = lax.broadcasted_iota(jnp.int32, (1, LANES), 1)
    low_lanes = lane128 < RET_QK_DIM
    row128 = lax.broadcasted_iota(jnp.int32, (LANES, 1), 0)
    low_rows = row128 < RET_QK_DIM
    nt = (((1,), (1,)), ((), ()))
    tn = (((0,), (0,)), ((), ()))
    view_rows = RET_CHUNK // MAX_DIL
    pitch = stage_ref.shape[1] // MAX_DIL

    for c in range(x_ref.shape[0] // RET_CHUNK):
        rows = slice(c * RET_CHUNK, (c + 1) * RET_CHUNK)
        h = _rms(x_ref[rows, :], gain_ref[...]).astype(BF16)
        proj = _dot_resident(h, w_ref)

        for t, out_ref in enumerate((qa_ref, ka_ref, va_ref)):
            for hp in range(N_PAIRS):
                col = t * N_PAIRS + hp
                block = proj[:, col * LANES:(col + 1) * LANES]
                for g in range(RET_CHUNK // SUBLANES):
                    m = c * view_rows + g * SUBLANES // MAX_DIL
                    r0 = g * SUBLANES % MAX_DIL
                    stage_ref[col, pl.ds(r0 * pitch + m, SUBLANES, stride=pitch), :] = (
                        block[g * SUBLANES:(g + 1) * SUBLANES])
                for r in range(MAX_DIL):
                    piece = stage_ref[col, r * pitch + c * view_rows:r * pitch + (c + 1) * view_rows, :]
                    out_ref[hp, c * view_rows:(c + 1) * view_rows, r * LANES:(r + 1) * LANES] = piece.astype(BF16)

        def rotary(t):
            swapped = jnp.where(first_half,
                                pltpu.roll(t, RET_QK_WIDTH - RET_QK_DIM // 2, axis=1),
                                pltpu.roll(t, RET_QK_DIM // 2, axis=1))
            cos = jnp.concatenate([cos_ref[rows, :]] * (RET_QK_WIDTH // LANES), axis=1)
            sin = jnp.concatenate([sin_ref[rows, :]] * (RET_QK_WIDTH // LANES), axis=1)
            return t * cos + swapped * sin

        qr = rotary(proj[:, base:base + RET_QK_WIDTH])
        kr = rotary(proj[:, base + RET_QK_WIDTH:base + 2 * RET_QK_WIDTH])

        for p in range(N_RET_HEADS // 2):
            cols = slice(p * LANES, (p + 1) * LANES)
            q2 = qr[:, cols]
            k2 = kr[:, cols]
            q_heads = (jnp.where(low_lanes, q2, 0.0), jnp.where(low_lanes, 0.0, q2))
            lhs = jnp.concatenate(q_heads, axis=0).astype(BF16)
            scores = lax.dot_general(lhs, k2.astype(BF16), nt, preferred_element_type=F32)
            kdec = (k2 * kdec_ref[:, cols]).astype(BF16)
            state = state_ref[p]
            state_bf = state.astype(BF16)
            kv = []
            for hh in range(2):
                head = 2 * p + hh
                v_h = proj[:, vbase + head * RET_V_DIM:vbase + (head + 1) * RET_V_DIM].astype(BF16)
                s_h = (scores[hh * RET_CHUNK:(hh + 1) * RET_CHUNK] * dmat_ref[head]).astype(BF16)
                inner = jnp.dot(s_h, v_h, preferred_element_type=F32)
                q_dec = (q_heads[hh] * qdec_ref[:, cols]).astype(BF16)
                cross = jnp.dot(q_dec, state_bf, preferred_element_type=F32)
                o = inner + cross
                o = o * lax.rsqrt(jnp.mean(o * o, axis=-1, keepdims=True) + NORM_EPS)
                g = proj[:, gbase + head * RET_V_DIM:gbase + (head + 1) * RET_V_DIM]
                o = o * (g * (1.0 / (1.0 + jnp.exp(-g))))
                ret_ref[rows, head * RET_V_DIM:(head + 1) * RET_V_DIM] = o.astype(BF16)
                kv.append(lax.dot_general(kdec, v_h, tn, preferred_element_type=F32))
            state_ref[p] = sdec_ref[p] * state + jnp.where(low_rows, kv[0], kv[1])


def _retention_constants(seq):
    freqs = ROPE_BASE ** (-np.arange(0, RET_QK_DIM, 2, dtype=np.float64) / RET_QK_DIM)
    ang = np.arange(seq, dtype=np.float64)[:, None] * freqs[None]
    cos_h = np.concatenate([np.cos(ang), np.cos(ang)], axis=1)
    sin_h = np.concatenate([-np.sin(ang), np.sin(ang)], axis=1)
    cos_t = np.tile(cos_h, (1, LANES // RET_QK_DIM)).astype(np.float32)
    sin_t = np.tile(sin_h, (1, LANES // RET_QK_DIM)).astype(np.float32)
    log_gamma = np.log1p(-np.exp2(-5.0 - np.arange(N_RET_HEADS, dtype=np.float64)))
    idx = np.arange(RET_CHUNK, dtype=np.float64)
    rel = idx[:, None] - idx[None, :]
    dmat = np.where(rel >= 0, np.exp(log_gamma[:, None, None] * np.maximum(rel, 0.0)), 0.0)
    qdec = np.exp(log_gamma[None, :] * (idx + 1.0)[:, None])
    kdec = np.exp(log_gamma[None, :] * (RET_CHUNK - 1 - idx)[:, None])
    qdec = np.repeat(qdec, RET_QK_DIM, axis=1)
    kdec = np.repeat(kdec, RET_QK_DIM, axis=1)
    sdec = np.repeat(np.exp(log_gamma * RET_CHUNK), RET_QK_DIM).reshape(N_RET_HEADS // 2, LANES, 1)
    sdec = np.broadcast_to(sdec, (N_RET_HEADS // 2, LANES, LANES))
    f = lambda a: jnp.asarray(np.ascontiguousarray(a), dtype=F32)
    return f(cos_t), f(sin_t), f(dmat), f(qdec), f(kdec), f(sdec)


def _inproj_retention(x, gain, w_in_bf):
    b, s, d = x.shape
    tm = INPROJ_TILE
    in_width = w_in_bf.shape[1]
    cos_t, sin_t, dmat, qdec, kdec, sdec = _retention_constants(s)
    const = lambda shape: pl.BlockSpec(shape, lambda bi, i: (0,) * len(shape))
    pair_spec = pl.BlockSpec((None, N_PAIRS, tm // MAX_DIL, MAX_DIL * LANES), lambda bi, i: (bi, 0, i, 0))
    pair_shape = jax.ShapeDtypeStruct((b, N_PAIRS, s // MAX_DIL, MAX_DIL * LANES), BF16)
    return pl.pallas_call(
        _inproj_retention_kernel,
        grid=(b, s // tm),
        in_specs=[
            pl.BlockSpec((None, tm, d), lambda bi, i: (bi, i, 0)),
            const((1, d)),
            const((d, in_width)),
            pl.BlockSpec((tm, LANES), lambda bi, i: (i, 0)),
            pl.BlockSpec((tm, LANES), lambda bi, i: (i, 0)),
            const((N_RET_HEADS, RET_CHUNK, RET_CHUNK)),
            const((RET_CHUNK, RET_QK_WIDTH)),
            const((RET_CHUNK, RET_QK_WIDTH)),
            const((N_RET_HEADS // 2, LANES, LANES)),
        ],
        out_specs=[pair_spec, pair_spec, pair_spec,
                   pl.BlockSpec((None, tm, RET_V_WIDTH), lambda bi, i: (bi, i, 0))],
        out_shape=[pair_shape, pair_shape, pair_shape, jax.ShapeDtypeStruct((b, s, RET_V_WIDTH), BF16)],
        scratch_shapes=[pltpu.VMEM((N_RET_HEADS // 2, LANES, LANES), F32),
                        pltpu.VMEM((3 * N_PAIRS, MAX_DIL * (tm // MAX_DIL + SUBLANES), LANES), F32)],
        compiler_params=pltpu.CompilerParams(dimension_semantics=("arbitrary", "arbitrary"),
                                             vmem_limit_bytes=VMEM_LIMIT_BYTES),
        name="inproj_retention",
    )(x, gain, w_in_bf, cos_t, sin_t, dmat, qdec, kdec, sdec)


def _attention_biases():
    q = np.arange(ATTN_ROWS)[None, :]
    k = np.arange(2 * ATTN_ROWS)[:, None]
    out = []
    out += [(ATTN_ROWS + q - k, k >= ATTN_ROWS)]
    out += [(4 * (32 + q % 32 - k % 64) + q // 32 - k // 64, k % 64 >= 32)]
    out += [(16 * (8 + q % 8 - k % 16) + q // 8 - k // 16, k % 16 >= 8)]
    biases = []
    for dist, in_seq in out:
        band = (dist >= 0) & (dist <= ATTN_STEPS)
        biases.append(np.where(band, 0.0, MASK_VALUE))
        biases.append(np.where(band & in_seq, 0.0, MASK_VALUE))
    return jnp.asarray(np.stack(biases), dtype=BF16)


def _attention_kernel(bias_ref, q_ref, k_ref, v_ref, o_ref, *scratch):
    first_tile = pl.program_id(2) == 0
    for hs in range(ATTN_PAIRS):
        _attention_pair(first_tile, bias_ref, q_ref.at[hs], k_ref.at[hs], v_ref.at[hs], o_ref.at[hs],
                        *(ref.at[hs] for ref in scratch))


def _attention_pair(first_tile, bias_ref, q_ref, k_ref, v_ref, o_ref, q0_s, q1_s, k_s, v_s, num_s, den_s, max_s):
    rows = ATTN_ROWS

    @pl.when(first_tile)
    def _():
        k_s[:, 0:rows, :] = jnp.zeros((MAX_DIL, rows, LANES), F32)
        v_s[:, 0:rows, :] = jnp.zeros((MAX_DIL, rows, LANES), F32)

    lane = lax.broadcasted_iota(jnp.int32, (1, LANES), 1)
    low = lane < ATTN_HEAD_DIM
    for r in range(MAX_DIL):
        cols = slice(r * LANES, (r + 1) * LANES)
        qf = q_ref[:, cols].astype(F32)
        q0_s[r] = jnp.where(low, qf, 0.0)
        q1_s[r] = jnp.where(low, 0.0, qf)
        k_s[r, rows:2 * rows, :] = k_ref[:, cols].astype(F32)
        v_s[r, rows:2 * rows, :] = v_ref[:, cols].astype(F32)

    ones = jnp.ones((2 * rows, LANES), BF16)
    pick = (lax.broadcasted_iota(jnp.int32, (2 * rows, LANES), 0) % rows
            == lax.broadcasted_iota(jnp.int32, (2 * rows, LANES), 1))
    pick = jnp.where(pick, 1.0, 0.0).astype(BF16)
    nt = (((1,), (1,)), ((), ()))

    def attend(q0_pieces, q1_pieces, k_pieces, v_pieces, bias):
        q2 = jnp.concatenate(q0_pieces + q1_pieces, axis=0).astype(BF16)
        k2 = jnp.concatenate(k_pieces, axis=0).astype(BF16)
        v2 = jnp.concatenate(v_pieces, axis=0).astype(BF16)
        s = lax.dot_general(jnp.concatenate([q2, pick], axis=1), jnp.concatenate([k2, bias], axis=1), nt,
                            preferred_element_type=F32)
        m = jnp.max(s, axis=-1, keepdims=True)
        p = jnp.exp2(s - m).astype(BF16)
        pv = jnp.dot(p, jnp.concatenate([v2, ones], axis=1), preferred_element_type=F32)
        return (jnp.where(low, pv[:rows, :LANES], pv[rows:, :LANES]),
                jnp.where(low, pv[:rows, LANES:], pv[rows:, LANES:]),
                jnp.where(low, m[:rows], m[rows:]))

    def variant(cond):
        return jnp.where(cond, 1, 0)

    def store(slot, pieces, results):
        for res, rows_dst, rows_src in pieces:
            for ref, val in zip((num_s, den_s, max_s), results):
                ref[slot, res, rows_dst, :] = val[rows_src]

    for c in range(4):
        for j in range(4):
            qrow, krow = slice(32 * j, 32 * j + 32), slice(96 + 32 * j, 160 + 32 * j)
            res = attend([q0_s[4 * a + c, qrow, :] for a in range(4)],
                         [q1_s[4 * a + c, qrow, :] for a in range(4)],
                         [k_s[4 * a + c, krow, :] for a in range(4)],
                         [v_s[4 * a + c, krow, :] for a in range(4)],
                         bias_ref[2 + variant(first_tile)] if j == 0 else bias_ref[2])
            store(0, [(4 * a + c, qrow, slice(32 * a, 32 * a + 32)) for a in range(4)], res)

    for j in range(MAX_DIL):
        qrow, krow = slice(8 * j, 8 * j + 8), slice(120 + 8 * j, 136 + 8 * j)
        res = attend([q0_s[r, qrow, :] for r in range(MAX_DIL)],
                     [q1_s[r, qrow, :] for r in range(MAX_DIL)],
                     [k_s[r, krow, :] for r in range(MAX_DIL)],
                     [v_s[r, krow, :] for r in range(MAX_DIL)],
                     bias_ref[4 + variant(first_tile)] if j == 0 else bias_ref[4])
        store(1, [(r, qrow, slice(8 * r, 8 * r + 8)) for r in range(MAX_DIL)], res)

    for r in range(MAX_DIL):
        n0, d0, m0 = attend([q0_s[r]], [q1_s[r]], [k_s[r]], [v_s[r]], bias_ref[variant(first_tile)])
        m1, m2 = max_s[0, r], max_s[1, r]
        top = jnp.maximum(jnp.maximum(m0, m1), m2)
        e0, e1, e2 = jnp.exp2(m0 - top), jnp.exp2(m1 - top), jnp.exp2(m2 - top)
        num = e0 * n0 + e1 * num_s[0, r] + e2 * num_s[1, r]
        den = e0 * d0 + e1 * den_s[0, r] + e2 * den_s[1, r]
        o_ref[:, r * LANES:(r + 1) * LANES] = (num * (1.0 / den)).astype(BF16)
        k_s[r, 0:rows, :] = k_s[r, rows:2 * rows, :]
        v_s[r, 0:rows, :] = v_s[r, rows:2 * rows, :]


def _dilated_attention(qa, ka, va):
    b, n_pairs, view_len, width = qa.shape
    tile = pl.BlockSpec((None, ATTN_PAIRS, ATTN_ROWS, width), lambda bi, hp, i: (bi, hp, i, 0))
    biases = _attention_biases()
    scratch = lambda n: pltpu.VMEM((ATTN_PAIRS, MAX_DIL, n, LANES), F32)
    return pl.pallas_call(
        _attention_kernel,
        grid=(b, n_pairs // ATTN_PAIRS, view_len // ATTN_ROWS),
        in_specs=[pl.BlockSpec(biases.shape, lambda bi, hp, i: (0, 0, 0)), tile, tile, tile],
        out_specs=tile,
        out_shape=jax.ShapeDtypeStruct(qa.shape, BF16),
        scratch_shapes=[scratch(ATTN_ROWS), scratch(ATTN_ROWS), scratch(2 * ATTN_ROWS), scratch(2 * ATTN_ROWS)]
        + [pltpu.VMEM((ATTN_PAIRS, len(DILATIONS) - 1, MAX_DIL, ATTN_ROWS, LANES), F32)] * 3,
        compiler_params=pltpu.CompilerParams(dimension_semantics=("arbitrary", "arbitrary", "arbitrary"),
                                             vmem_limit_bytes=VMEM_LIMIT_BYTES),
        name="dilated_attention",
    )(biases, qa, ka, va)


def _outproj_ffn_kernel(x_ref, attn_ref, ret_ref, gpost_ref, gpre_ref, gffn_ref, wout_ref, wup_ref, cw_ref, cb_ref,
                        wdown_ref, o_ref, act_s, halo_s, stage_s):
    tm = x_ref.shape[0]
    d_ff = wdown_ref.shape[0]

    @pl.when(pl.program_id(1) == 0)
    def _():
        halo_s[...] = jnp.zeros_like(halo_s)

    first_rows = lax.broadcasted_iota(jnp.int32, (SUBLANES, 1), 0)
    hm = tm // FF_STREAMS
    vm = hm // MAX_DIL
    x1, hn = [], []
    for st in range(FF_STREAMS):
        rows = slice(st * hm, (st + 1) * hm)
        for hp in range(N_PAIRS):
            for r in range(MAX_DIL):
                piece = attn_ref[hp, st * vm:(st + 1) * vm, r * LANES:(r + 1) * LANES].astype(F32)
                stage_s[hp, pl.ds(st * hm + r, vm, stride=MAX_DIL), :] = piece
        mixed = jnp.concatenate([stage_s[hp, rows, :].astype(BF16) for hp in range(N_PAIRS)] + [ret_ref[rows, :]],
                                axis=-1)
        mix = _dot_resident(mixed, wout_ref)
        x1.append(x_ref[rows, :] + _rms(mix, gpost_ref[...]))
        hn.append(_rms(x1[st], gpre_ref[...]).astype(BF16))

    def conv(cols, u):
        halo = halo_s[:, cols]
        halo_s[:, cols] = u[hm - SUBLANES:hm]
        taps = []
        for shift in (1, 2):
            rolled = pltpu.roll(u, shift, axis=0)
            head = jnp.where(first_rows < shift, pltpu.roll(halo, shift, axis=0), rolled[0:SUBLANES])
            taps.append(jnp.concatenate([head, rolled[SUBLANES:]], axis=0))
        return cb_ref[:, cols] + cw_ref[2:3, cols] * u + cw_ref[1:2, cols] * taps[0] + cw_ref[0:1, cols] * taps[1]

    for c in range(d_ff // FF_CHUNK):
        gcols = slice(c * FF_CHUNK, (c + 1) * FF_CHUNK)
        ucols = slice(d_ff + c * FF_CHUNK, d_ff + (c + 1) * FF_CHUNK)
        for st in range(FF_STREAMS):
            gate = conv(gcols, jnp.dot(hn[st], wup_ref[:, gcols], preferred_element_type=F32))
            half_up = conv(ucols, jnp.dot(hn[st], wup_ref[:, ucols], preferred_element_type=F32))
            inner = gate * (GELU_C1 + GELU_C3 * (gate * gate))
            act_s[st * hm:(st + 1) * hm, gcols] = (gate * (1.0 + jnp.tanh(inner)) * half_up).astype(BF16)

    for st in range(FF_STREAMS):
        rows = slice(st * hm, (st + 1) * hm)
        ffn = _dot_resident(act_s[rows, :], wdown_ref)
        o_ref[rows, :] = x1[st] + _rms(ffn, gffn_ref[...])


def _outproj_ffn(x, attn, ret, g_post, g_pre, g_ffn, w_out_bf, w_up_bf, conv_w, conv_b, w_down_bf):
    b, s, d = x.shape
    tm = TOKEN_TILE
    d_ff = w_down_bf.shape[0]
    const = lambda shape: pl.BlockSpec(shape, lambda bi, i: (0,) * len(shape), pipeline_mode=pl.Buffered(1))
    return pl.pallas_call(
        _outproj_ffn_kernel,
        grid=(b, s // tm),
        in_specs=[
            pl.BlockSpec((None, tm, d), lambda bi, i: (bi, i, 0)),
            pl.BlockSpec((None, N_PAIRS, tm // MAX_DIL, MAX_DIL * LANES), lambda bi, i: (bi, 0, i, 0)),
            pl.BlockSpec((None, tm, RET_V_WIDTH), lambda bi, i: (bi, i, 0)),
            const((1, d)), const((1, d)), const((1, d)),
            const(w_out_bf.shape), const(w_up_bf.shape), const(conv_w.shape), const(conv_b.shape),
            const(w_down_bf.shape),
        ],
        out_specs=pl.BlockSpec((None, tm, d), lambda bi, i: (bi, i, 0)),
        out_shape=jax.ShapeDtypeStruct((b, s, d), F32),
        scratch_shapes=[pltpu.VMEM((tm, d_ff), BF16),
                        pltpu.VMEM((SUBLANES, 2 * d_ff), F32),
                        pltpu.VMEM((N_PAIRS, tm, LANES), F32)],
        compiler_params=pltpu.CompilerParams(dimension_semantics=("arbitrary", "arbitrary"),
                                             vmem_limit_bytes=VMEM_LIMIT_BYTES),
        name="outproj_ffn",
    )(x, attn, ret, g_post, g_pre, g_ffn, w_out_bf, w_up_bf, conv_w, conv_b, w_down_bf)


def _layer(x, g_mix_pre, g_mix_post, w_in, w_out, g_ffn_pre, g_ffn_post, w_up, conv_w, conv_b, w_down):
    d = x.shape[-1]
    d_ff = w_down.shape[0]
    col_scale = np.ones((w_in.shape[1],), np.float32)
    col_scale[:ATTN_WIDTH] = ATTN_HEAD_DIM ** -0.5 * np.log2(np.e)
    kr0 = 3 * ATTN_WIDTH + RET_QK_WIDTH
    col_scale[kr0:kr0 + RET_QK_WIDTH] = RET_QK_DIM ** -0.5
    w_in_bf = _pad_columns((w_in * col_scale).astype(BF16))
    qa, ka, va, ret = _inproj_retention(x, g_mix_pre.reshape(1, d), w_in_bf)
    attn = _dilated_attention(qa, ka, va)
    up_half = np.concatenate([np.ones((d_ff,), np.float32), np.full((d_ff,), 0.5, np.float32)])
    return _outproj_ffn(x, attn, ret, g_mix_post.reshape(1, d), g_ffn_pre.reshape(1, d), g_ffn_post.reshape(1, d),
                        _pad_columns(w_out.astype(BF16)), _pad_columns(w_up.astype(BF16)), conv_w * up_half,
                        conv_b.reshape(1, -1) * up_half, _pad_columns(w_down.astype(BF16)))


def kernel(x, mix_pre_gain, mix_post_gain, w_in, w_out, ffn_pre_gain, ffn_post_gain, w_up, conv_w, conv_b, w_down):
    for layer in range(mix_pre_gain.shape[0]):
        x = _layer(x, mix_pre_gain[layer], mix_post_gain[layer], w_in[layer], w_out[layer], ffn_pre_gain[layer],
                   ffn_post_gain[layer], w_up[layer], conv_w[layer], conv_b[layer], w_down[layer])
    return x
```

```python
import numpy as np
import jax
import jax.numpy as jnp
from jax import lax
from jax.experimental import pallas as pl
from jax.experimental.pallas import tpu as pltpu

F32 = jnp.float32
BF16 = jnp.bfloat16

ATTN_HEAD_DIM = 64
N_ATTN_HEADS = 8
ATTN_WIDTH = N_ATTN_HEADS * ATTN_HEAD_DIM
ATTN_STEPS = 128
DILATIONS = (1, 4, 16)
RET_V_DIM = 128
N_RET_HEADS = 4
RET_QK_DIM = 64
RET_QK_WIDTH = N_RET_HEADS * RET_QK_DIM
RET_V_WIDTH = N_RET_HEADS * RET_V_DIM
ROPE_BASE = 10000.0
CONV_WIDTH = 3
NORM_EPS = 1e-6
MASK_VALUE = -1e30

LANES = 128
SUBLANES = 8
HEAD_PAIR = LANES // ATTN_HEAD_DIM
N_PAIRS = N_ATTN_HEADS // HEAD_PAIR
MAX_DIL = max(DILATIONS)
ATTN_ROWS = 128
TOKEN_TILE = 1024
INPROJ_TILE = 1024
RET_CHUNK = 256
ATTN_PAIRS = 2
FF_CHUNK = 256
FF_STREAMS = 4
GELU_C1 = float(np.sqrt(2.0 / np.pi).astype(np.float32))
GELU_C3 = float(np.float32(GELU_C1 * 0.044715))
VMEM_LIMIT_BYTES = 56 * 1024 * 1024


def _pad_columns(w):
    return jnp.concatenate([w, jnp.zeros((w.shape[0], LANES), w.dtype)], axis=1)


def _dot_resident(lhs, w_ref):
    return jnp.dot(lhs, w_ref[:, :w_ref.shape[1] - LANES], preferred_element_type=F32)


def _rms(x, gain):
    return x * lax.rsqrt(jnp.mean(x * x, axis=-1, keepdims=True) + NORM_EPS) * gain


def _inproj_retention_kernel(x_ref, gain_ref, w_ref, cos_ref, sin_ref, dmat_ref, qdec_ref, kdec_ref, sdec_ref,
                             qa_ref, ka_ref, va_ref, ret_ref, state_ref, stage_ref):
    @pl.when(pl.program_id(1) == 0)
    def _():
        state_ref[...] = jnp.zeros_like(state_ref)

    base = 3 * ATTN_WIDTH
    vbase = base + 2 * RET_QK_WIDTH
    gbase = vbase + RET_V_WIDTH
    lane = lax.broadcasted_iota(jnp.int32, (1, RET_QK_WIDTH), 1)
    first_half = (lane % RET_QK_DIM) < (RET_QK_DIM // 2)
    lane128 = lax.broadcasted_iota(jnp.int32, (1, LANES), 1)
    low_lanes = lane128 < RET_QK_DIM
    row128 = lax.broadcasted_iota(jnp.int32, (LANES, 1), 0)
    low_rows = row128 < RET_QK_DIM
    nt = (((1,), (1,)), ((), ()))
    tn = (((0,), (0,)), ((), ()))
    view_rows = RET_CHUNK // MAX_DIL
    pitch = stage_ref.shape[1] // MAX_DIL

    for c in range(x_ref.shape[0] // RET_CHUNK):
        rows = slice(c * RET_CHUNK, (c + 1) * RET_CHUNK)
        h = _rms(x_ref[rows, :], gain_ref[...]).astype(BF16)
        proj = _dot_resident(h, w_ref)

        for t, out_ref in enumerate((qa_ref, ka_ref, va_ref)):
            for hp in range(N_PAIRS):
                col = t * N_PAIRS + hp
                block = proj[:, col * LANES:(col + 1) * LANES]
                for g in range(RET_CHUNK // SUBLANES):
                    m = c * view_rows + g * SUBLANES // MAX_DIL
                    r0 = g * SUBLANES % MAX_DIL
                    stage_ref[col, pl.ds(r0 * pitch + m, SUBLANES, stride=pitch), :] = (
                        block[g * SUBLANES:(g + 1) * SUBLANES])
                for r in range(MAX_DIL):
                    piece = stage_ref[col, r * pitch + c * view_rows:r * pitch + (c + 1) * view_rows, :]
                    out_ref[hp, c * view_rows:(c + 1) * view_rows, r * LANES:(r + 1) * LANES] = piece.astype(BF16)

        def rotary(t):
            swapped = jnp.where(first_half,
                                pltpu.roll(t, RET_QK_WIDTH - RET_QK_DIM // 2, axis=1),
                                pltpu.roll(t, RET_QK_DIM // 2, axis=1))
            cos = jnp.concatenate([cos_ref[rows, :]] * (RET_QK_WIDTH // LANES), axis=1)
            sin = jnp.concatenate([sin_ref[rows, :]] * (RET_QK_WIDTH // LANES), axis=1)
            return t * cos + swapped * sin

        qr = rotary(proj[:, base:base + RET_QK_WIDTH])
        kr = rotary(proj[:, base + RET_QK_WIDTH:base + 2 * RET_QK_WIDTH])

        for p in range(N_RET_HEADS // 2):
            cols = slice(p * LANES, (p + 1) * LANES)
            q2 = qr[:, cols]
            k2 = kr[:, cols]
            q_heads = (jnp.where(low_lanes, q2, 0.0), jnp.where(low_lanes, 0.0, q2))
            lhs = jnp.concatenate(q_heads, axis=0).astype(BF16)
            scores = lax.dot_general(lhs, k2.astype(BF16), nt, preferred_element_type=F32)
            kdec = (k2 * kdec_ref[:, cols]).astype(BF16)
            state = state_ref[p]
            state_bf = state.astype(BF16)
            kv = []
            for hh in range(2):
                head = 2 * p + hh
                v_h = proj[:, vbase + head * RET_V_DIM:vbase + (head + 1) * RET_V_DIM].astype(BF16)
                s_h = (scores[hh * RET_CHUNK:(hh + 1) * RET_CHUNK] * dmat_ref[head]).astype(BF16)
                inner = jnp.dot(s_h, v_h, preferred_element_type=F32)
                q_dec = (q_heads[hh] * qdec_ref[:, cols]).astype(BF16)
                cross = jnp.dot(q_dec, state_bf, preferred_element_type=F32)
                o = inner + cross
                o = o * lax.rsqrt(jnp.mean(o * o, axis=-1, keepdims=True) + NORM_EPS)
                g = proj[:, gbase + head * RET_V_DIM:gbase + (head + 1) * RET_V_DIM]
                o = o * (g * (1.0 / (1.0 + jnp.exp(-g))))
                ret_ref[rows, head * RET_V_DIM:(head + 1) * RET_V_DIM] = o.astype(BF16)
                kv.append(lax.dot_general(kdec, v_h, tn, preferred_element_type=F32))
            state_ref[p] = sdec_ref[p] * state + jnp.where(low_rows, kv[0], kv[1])


def _retention_constants(seq):
    freqs = ROPE_BASE ** (-np.arange(0, RET_QK_DIM, 2, dtype=np.float64) / RET_QK_DIM)
    ang = np.arange(seq, dtype=np.float64)[:, None] * freqs[None]
    cos_h = np.concatenate([np.cos(ang), np.cos(ang)], axis=1)
    sin_h = np.concatenate([-np.sin(ang), np.sin(ang)], axis=1)
    cos_t = np.tile(cos_h, (1, LANES // RET_QK_DIM)).astype(np.float32)
    sin_t = np.tile(sin_h, (1, LANES // RET_QK_DIM)).astype(np.float32)
    log_gamma = np.log1p(-np.exp2(-5.0 - np.arange(N_RET_HEADS, dtype=np.float64)))
    idx = np.arange(RET_CHUNK, dtype=np.float64)
    rel = idx[:, None] - idx[None, :]
    dmat = np.where(rel >= 0, np.exp(log_gamma[:, None, None] * np.maximum(rel, 0.0)), 0.0)
    qdec = np.exp(log_gamma[None, :] * (idx + 1.0)[:, None])
    kdec = np.exp(log_gamma[None, :] * (RET_CHUNK - 1 - idx)[:, None])
    qdec = np.repeat(qdec, RET_QK_DIM, axis=1)
    kdec = np.repeat(kdec, RET_QK_DIM, axis=1)
    sdec = np.repeat(np.exp(log_gamma * RET_CHUNK), RET_QK_DIM).reshape(N_RET_HEADS // 2, LANES, 1)
    sdec = np.broadcast_to(sdec, (N_RET_HEADS // 2, LANES, LANES))
    f = lambda a: jnp.asarray(np.ascontiguousarray(a), dtype=F32)
    return f(cos_t), f(sin_t), f(dmat), f(qdec), f(kdec), f(sdec)


def _inproj_retention(x, gain, w_in_bf):
    b, s, d = x.shape
    tm = INPROJ_TILE
    in_width = w_in_bf.shape[1]
    cos_t, sin_t, dmat, qdec, kdec, sdec = _retention_constants(s)
    const = lambda shape: pl.BlockSpec(shape, lambda bi, i: (0,) * len(shape))
    pair_spec = pl.BlockSpec((None, N_PAIRS, tm // MAX_DIL, MAX_DIL * LANES), lambda bi, i: (bi, 0, i, 0))
    pair_shape = jax.ShapeDtypeStruct((b, N_PAIRS, s // MAX_DIL, MAX_DIL * LANES), BF16)
    return pl.pallas_call(
        _inproj_retention_kernel,
        grid=(b, s // tm),
        in_specs=[
            pl.BlockSpec((None, tm, d), lambda bi, i: (bi, i, 0)),
            const((1, d)),
            const((d, in_width)),
            pl.BlockSpec((tm, LANES), lambda bi, i: (i, 0)),
            pl.BlockSpec((tm, LANES), lambda bi, i: (i, 0)),
            const((N_RET_HEADS, RET_CHUNK, RET_CHUNK)),
            const((RET_CHUNK, RET_QK_WIDTH)),
            const((RET_CHUNK, RET_QK_WIDTH)),
            const((N_RET_HEADS // 2, LANES, LANES)),
        ],
        out_specs=[pair_spec, pair_spec, pair_spec,
                   pl.BlockSpec((None, tm, RET_V_WIDTH), lambda bi, i: (bi, i, 0))],
        out_shape=[pair_shape, pair_shape, pair_shape, jax.ShapeDtypeStruct((b, s, RET_V_WIDTH), BF16)],
        scratch_shapes=[pltpu.VMEM((N_RET_HEADS // 2, LANES, LANES), F32),
                        pltpu.VMEM((3 * N_PAIRS, MAX_DIL * (tm // MAX_DIL + SUBLANES), LANES), F32)],
        compiler_params=pltpu.CompilerParams(dimension_semantics=("arbitrary", "arbitrary"),
                                             vmem_limit_bytes=VMEM_LIMIT_BYTES),
        name="inproj_retention",
    )(x, gain, w_in_bf, cos_t, sin_t, dmat, qdec, kdec, sdec)


def _attention_biases():
    q = np.arange(ATTN_ROWS)[None, :]
    k = np.arange(2 * ATTN_ROWS)[:, None]
    out = []
    out += [(ATTN_ROWS + q - k, k >= ATTN_ROWS)]
    out += [(4 * (32 + q % 32 - k % 64) + q // 32 - k // 64, k % 64 >= 32)]
    out += [(16 * (8 + q % 8 - k % 16) + q // 8 - k // 16, k % 16 >= 8)]
    biases = []
    for dist, in_seq in out:
        band = (dist >= 0) & (dist <= ATTN_STEPS)
        biases.append(np.where(band, 0.0, MASK_VALUE))
        biases.append(np.where(band & in_seq, 0.0, MASK_VALUE))
    return jnp.asarray(np.stack(biases), dtype=BF16)


def _attention_kernel(bias_ref, q_ref, k_ref, v_ref, o_ref, *scratch):
    first_tile = pl.program_id(2) == 0
    for hs in range(ATTN_PAIRS):
        _attention_pair(first_tile, bias_ref, q_ref.at[hs], k_ref.at[hs], v_ref.at[hs], o_ref.at[hs],
                        *(ref.at[hs] for ref in scratch))


def _attention_pair(first_tile, bias_ref, q_ref, k_ref, v_ref, o_ref, q0_s, q1_s, k_s, v_s, num_s, den_s, max_s):
    rows = ATTN_ROWS

    @pl.when(first_tile)
    def _():
        k_s[:, 0:rows, :] = jnp.zeros((MAX_DIL, rows, LANES), F32)
        v_s[:, 0:rows, :] = jnp.zeros((MAX_DIL, rows, LANES), F32)

    lane = lax.broadcasted_iota(jnp.int32, (1, LANES), 1)
    low = lane < ATTN_HEAD_DIM
    for r in range(MAX_DIL):
        cols = slice(r * LANES, (r + 1) * LANES)
        qf = q_ref[:, cols].astype(F32)
        q0_s[r] = jnp.where(low, qf, 0.0)
        q1_s[r] = jnp.where(low, 0.0, qf)
        k_s[r, rows:2 * rows, :] = k_ref[:, cols].astype(F32)
        v_s[r, rows:2 * rows, :] = v_ref[:, cols].astype(F32)

    ones = jnp.ones((2 * rows, LANES), BF16)
    pick = (lax.broadcasted_iota(jnp.int32, (2 * rows, LANES), 0) % rows
            == lax.broadcasted_iota(jnp.int32, (2 * rows, LANES), 1))
    pick = jnp.where(pick, 1.0, 0.0).astype(BF16)
    nt = (((1,), (1,)), ((), ()))

    def attend(q0_pieces, q1_pieces, k_pieces, v_pieces, bias):
        q2 = jnp.concatenate(q0_pieces + q1_pieces, axis=0).astype(BF16)
        k2 = jnp.concatenate(k_pieces, axis=0).astype(BF16)
        v2 = jnp.concatenate(v_pieces, axis=0).astype(BF16)
        s = lax.dot_general(jnp.concatenate([q2, pick], axis=1), jnp.concatenate([k2, bias], axis=1), nt,
                            preferred_element_type=F32)
        m = jnp.max(s, axis=-1, keepdims=True)
        p = jnp.exp2(s - m).astype(BF16)
        pv = jnp.dot(p, jnp.concatenate([v2, ones], axis=1), preferred_element_type=F32)
        return (jnp.where(low, pv[:rows, :LANES], pv[rows:, :LANES]),
                jnp.where(low, pv[:rows, LANES:], pv[rows:, LANES:]),
                jnp.where(low, m[:rows], m[rows:]))

    def variant(cond):
        return jnp.where(cond, 1, 0)

    def store(slot, pieces, results):
        for res, rows_dst, rows_src in pieces:
            for ref, val in zip((num_s, den_s, max_s), results):
                ref[slot, res, rows_dst, :] = val[rows_src]

    for c in range(4):
        for j in range(4):
            qrow, krow = slice(32 * j, 32 * j + 32), slice(96 + 32 * j, 160 + 32 * j)
            res = attend([q0_s[4 * a + c, qrow, :] for a in range(4)],
                         [q1_s[4 * a + c, qrow, :] for a in range(4)],
                         [k_s[4 * a + c, krow, :] for a in range(4)],
                         [v_s[4 * a + c, krow, :] for a in range(4)],
                         bias_ref[2 + variant(first_tile)] if j == 0 else bias_ref[2])
            store(0, [(4 * a + c, qrow, slice(32 * a, 32 * a + 32)) for a in range(4)], res)

    for j in range(MAX_DIL):
        qrow, krow = slice(8 * j, 8 * j + 8), slice(120 + 8 * j, 136 + 8 * j)
        res = attend([q0_s[r, qrow, :] for r in range(MAX_DIL)],
                     [q1_s[r, qrow, :] for r in range(MAX_DIL)],
                     [k_s[r, krow, :] for r in range(MAX_DIL)],
                     [v_s[r, krow, :] for r in range(MAX_DIL)],
                     bias_ref[4 + variant(first_tile)] if j == 0 else bias_ref[4])
        store(1, [(r, qrow, slice(8 * r, 8 * r + 8)) for r in range(MAX_DIL)], res)

    for r in range(MAX_DIL):
        n0, d0, m0 = attend([q0_s[r]], [q1_s[r]], [k_s[r]], [v_s[r]], bias_ref[variant(first_tile)])
        m1, m2 = max_s[0, r], max_s[1, r]
        top = jnp.maximum(jnp.maximum(m0, m1), m2)
        e0, e1, e2 = jnp.exp2(m0 - top), jnp.exp2(m1 - top), jnp.exp2(m2 - top)
        num = e0 * n0 + e1 * num_s[0, r] + e2 * num_s[1, r]
        den = e0 * d0 + e1 * den_s[0, r] + e2 * den_s[1, r]
        o_ref[:, r * LANES:(r + 1) * LANES] = (num * (1.0 / den)).astype(BF16)
        k_s[r, 0:rows, :] = k_s[r, rows:2 * rows, :]
        v_s[r, 0:rows, :] = v_s[r, rows:2 * rows, :]


def _dilated_attention(qa, ka, va):
    b, n_pairs, view_len, width = qa.shape
    tile = pl.BlockSpec((None, ATTN_PAIRS, ATTN_ROWS, width), lambda bi, hp, i: (bi, hp, i, 0))
    biases = _attention_biases()
    scratch = lambda n: pltpu.VMEM((ATTN_PAIRS, MAX_DIL, n, LANES), F32)
    return pl.pallas_call(
        _attention_kernel,
        grid=(b, n_pairs // ATTN_PAIRS, view_len // ATTN_ROWS),
        in_specs=[pl.BlockSpec(biases.shape, lambda bi, hp, i: (0, 0, 0)), tile, tile, tile],
        out_specs=tile,
        out_shape=jax.ShapeDtypeStruct(qa.shape, BF16),
        scratch_shapes=[scratch(ATTN_ROWS), scratch(ATTN_ROWS), scratch(2 * ATTN_ROWS), scratch(2 * ATTN_ROWS)]
        + [pltpu.VMEM((ATTN_PAIRS, len(DILATIONS) - 1, MAX_DIL, ATTN_ROWS, LANES), F32)] * 3,
        compiler_params=pltpu.CompilerParams(dimension_semantics=("arbitrary", "arbitrary", "arbitrary"),
                                             vmem_limit_bytes=VMEM_LIMIT_BYTES),
        name="dilated_attention",
    )(biases, qa, ka, va)


def _outproj_ffn_kernel(x_ref, attn_ref, ret_ref, gpost_ref, gpre_ref, gffn_ref, wout_ref, wup_ref, cw_ref, cb_ref,
                        wdown_ref, o_ref, act_s, halo_s, stage_s):
    tm = x_ref.shape[0]
    d_ff = wdown_ref.shape[0]

    @pl.when(pl.program_id(1) == 0)
    def _():
        halo_s[...] = jnp.zeros_like(halo_s)

    first_rows = lax.broadcasted_iota(jnp.int32, (SUBLANES, 1), 0)
    hm = tm // FF_STREAMS
    vm = hm // MAX_DIL
    x1, hn = [], []
    for st in range(FF_STREAMS):
        rows = slice(st * hm, (st + 1) * hm)
        for hp in range(N_PAIRS):
            for r in range(MAX_DIL):
                piece = attn_ref[hp, st * vm:(st + 1) * vm, r * LANES:(r + 1) * LANES].astype(F32)
                stage_s[hp, pl.ds(st * hm + r, vm, stride=MAX_DIL), :] = piece
        mixed = jnp.concatenate([stage_s[hp, rows, :].astype(BF16) for hp in range(N_PAIRS)] + [ret_ref[rows, :]],
                                axis=-1)
        mix = _dot_resident(mixed, wout_ref)
        x1.append(x_ref[rows, :] + _rms(mix, gpost_ref[...]))
        hn.append(_rms(x1[st], gpre_ref[...]).astype(BF16))

    def conv(cols, u):
        halo = halo_s[:, cols]
        halo_s[:, cols] = u[hm - SUBLANES:hm]
        taps = []
        for shift in (1, 2):
            rolled = pltpu.roll(u, shift, axis=0)
            head = jnp.where(first_rows < shift, pltpu.roll(halo, shift, axis=0), rolled[0:SUBLANES])
            taps.append(jnp.concatenate([head, rolled[SUBLANES:]], axis=0))
        return cb_ref[:, cols] + cw_ref[2:3, cols] * u + cw_ref[1:2, cols] * taps[0] + cw_ref[0:1, cols] * taps[1]

    for c in range(d_ff // FF_CHUNK):
        gcols = slice(c * FF_CHUNK, (c + 1) * FF_CHUNK)
        ucols = slice(d_ff + c * FF_CHUNK, d_ff + (c + 1) * FF_CHUNK)
        for st in range(FF_STREAMS):
            gate = conv(gcols, jnp.dot(hn[st], wup_ref[:, gcols], preferred_element_type=F32))
            half_up = conv(ucols, jnp.dot(hn[st], wup_ref[:, ucols], preferred_element_type=F32))
            inner = gate * (GELU_C1 + GELU_C3 * (gate * gate))
            act_s[st * hm:(st + 1) * hm, gcols] = (gate * (1.0 + jnp.tanh(inner)) * half_up).astype(BF16)

    for st in range(FF_STREAMS):
        rows = slice(st * hm, (st + 1) * hm)
        ffn = _dot_resident(act_s[rows, :], wdown_ref)
        o_ref[rows, :] = x1[st] + _rms(ffn, gffn_ref[...])


def _outproj_ffn(x, attn, ret, g_post, g_pre, g_ffn, w_out_bf, w_up_bf, conv_w, conv_b, w_down_bf):
    b, s, d = x.shape
    tm = TOKEN_TILE
    d_ff = w_down_bf.shape[0]
    const = lambda shape: pl.BlockSpec(shape, lambda bi, i: (0,) * len(shape), pipeline_mode=pl.Buffered(1))
    return pl.pallas_call(
        _outproj_ffn_kernel,
        grid=(b, s // tm),
        in_specs=[
            pl.BlockSpec((None, tm, d), lambda bi, i: (bi, i, 0)),
            pl.BlockSpec((None, N_PAIRS, tm // MAX_DIL, MAX_DIL * LANES), lambda bi, i: (bi, 0, i, 0)),
            pl.BlockSpec((None, tm, RET_V_WIDTH), lambda bi, i: (bi, i, 0)),
            const((1, d)), const((1, d)), const((1, d)),
            const(w_out_bf.shape), const(w_up_bf.shape), const(conv_w.shape), const(conv_b.shape),
            const(w_down_bf.shape),
        ],
        out_specs=pl.BlockSpec((None, tm, d), lambda bi, i: (bi, i, 0)),
        out_shape=jax.ShapeDtypeStruct((b, s, d), F32),
        scratch_shapes=[pltpu.VMEM((tm, d_ff), BF16),
                        pltpu.VMEM((SUBLANES, 2 * d_ff), F32),
                        pltpu.VMEM((N_PAIRS, tm, LANES), F32)],
        compiler_params=pltpu.CompilerParams(dimension_semantics=("arbitrary", "arbitrary"),
                                             vmem_limit_bytes=VMEM_LIMIT_BYTES),
        name="outproj_ffn",
    )(x, attn, ret, g_post, g_pre, g_ffn, w_out_bf, w_up_bf, conv_w, conv_b, w_down_bf)


def _layer(x, g_mix_pre, g_mix_post, w_in, w_out, g_ffn_pre, g_ffn_post, w_up, conv_w, conv_b, w_down):
    d = x.shape[-1]
    d_ff = w_down.shape[0]
    col_scale = np.ones((w_in.shape[1],), np.float32)
    col_scale[:ATTN_WIDTH] = ATTN_HEAD_DIM ** -0.5 * np.log2(np.e)
    kr0 = 3 * ATTN_WIDTH + RET_QK_WIDTH
    col_scale[kr0:kr0 + RET_QK_WIDTH] = RET_QK_DIM ** -0.5
    w_in_bf = _pad_columns((w_in * col_scale).astype(BF16))
    qa, ka, va, ret = _inproj_retention(x, g_mix_pre.reshape(1, d), w_in_bf)
    attn = _dilated_attention(qa, ka, va)
    up_half = np.concatenate([np.ones((d_ff,), np.float32), np.full((d_ff,), 0.5, np.float32)])
    return _outproj_ffn(x, attn, ret, g_mix_post.reshape(1, d), g_ffn_pre.reshape(1, d), g_ffn_post.reshape(1, d),
                        _pad_columns(w_out.astype(BF16)), _pad_columns(w_up.astype(BF16)), conv_w * up_half,
                        conv_b.reshape(1, -1) * up_half, _pad_columns(w_down.astype(BF16)))


def kernel(x, mix_pre_gain, mix_post_gain, w_in, w_out, ffn_pre_gain, ffn_post_gain, w_up, conv_w, conv_b, w_down):
    for layer in range(mix_pre_gain.shape[0]):
        x = _layer(x, mix_pre_gain[layer], mix_post_gain[layer], w_in[layer], w_out[layer], ffn_pre_gain[layer],
                   ffn_post_gain[layer], w_up[layer], conv_w[layer], conv_b[layer], w_down[layer])
    return x
```

```python
import numpy as np
import jax
import jax.numpy as jnp
from jax import lax
from jax.experimental import pallas as pl
from jax.experimental.pallas import tpu as pltpu

F32 = jnp.float32
BF16 = jnp.bfloat16

ATTN_HEAD_DIM = 64
N_ATTN_HEADS = 8
ATTN_WIDTH = N_ATTN_HEADS * ATTN_HEAD_DIM
ATTN_STEPS = 128
DILATIONS = (1, 4, 16)
RET_V_DIM = 128
N_RET_HEADS = 4
RET_QK_DIM = 64
RET_QK_WIDTH = N_RET_HEADS * RET_QK_DIM
RET_V_WIDTH = N_RET_HEADS * RET_V_DIM
ROPE_BASE = 10000.0
CONV_WIDTH = 3
NORM_EPS = 1e-6
MASK_VALUE = -1e30

LANES = 128
SUBLANES = 8
HEAD_PAIR = LANES // ATTN_HEAD_DIM
N_PAIRS = N_ATTN_HEADS // HEAD_PAIR
MAX_DIL = max(DILATIONS)
ATTN_ROWS = 128
TOKEN_TILE = 1024
INPROJ_TILE = 1024
RET_CHUNK = 256
ATTN_PAIRS = 2
FF_CHUNK = 256
FF_STREAMS = 4
GELU_C1 = float(np.sqrt(2.0 / np.pi).astype(np.float32))
GELU_C3 = float(np.float32(GELU_C1 * 0.044715))
VMEM_LIMIT_BYTES = 56 * 1024 * 1024


def _pad_columns(w):
    return jnp.concatenate([w, jnp.zeros((w.shape[0], LANES), w.dtype)], axis=1)


def _dot_resident(lhs, w_ref):
    return jnp.dot(lhs, w_ref[:, :w_ref.shape[1] - LANES], preferred_element_type=F32)


def _rms(x, gain):
    return x * lax.rsqrt(jnp.mean(x * x, axis=-1, keepdims=True) + NORM_EPS) * gain


def _inproj_retention_kernel(x_ref, gain_ref, w_ref, cos_ref, sin_ref, dmat_ref, qdec_ref, kdec_ref, sdec_ref,
                             qa_ref, ka_ref, va_ref, ret_ref, state_ref, stage_ref):
    @pl.when(pl.program_id(1) == 0)
    def _():
        state_ref[...] = jnp.zeros_like(state_ref)

    base = 3 * ATTN_WIDTH
    vbase = base + 2 * RET_QK_WIDTH
    gbase = vbase + RET_V_WIDTH
    lane = lax.broadcasted_iota(jnp.int32, (1, RET_QK_WIDTH), 1)
    first_half = (lane % RET_QK_DIM) < (RET_QK_DIM // 2)
    lane128 = lax.broadcasted_iota(jnp.int32, (1, LANES), 1)
    low_lanes = lane128 < RET_QK_DIM
    row128 = lax.broadcasted_iota(jnp.int32, (LANES, 1), 0)
    low_rows = row128 < RET_QK_DIM
    nt = (((1,), (1,)), ((), ()))
    tn = (((0,), (0,)), ((), ()))
    view_rows = RET_CHUNK // MAX_DIL
    pitch = stage_ref.shape[1] // MAX_DIL

    for c in range(x_ref.shape[0] // RET_CHUNK):
        rows = slice(c * RET_CHUNK, (c + 1) * RET_CHUNK)
        h = _rms(x_ref[rows, :], gain_ref[...]).astype(BF16)
        proj = _dot_resident(h, w_ref)

        for t, out_ref in enumerate((qa_ref, ka_ref, va_ref)):
            for hp in range(N_PAIRS):
                col = t * N_PAIRS + hp
                block = proj[:, col * LANES:(col + 1) * LANES]
                for g in range(RET_CHUNK // SUBLANES):
                    m = c * view_rows + g * SUBLANES // MAX_DIL
                    r0 = g * SUBLANES % MAX_DIL
                    stage_ref[col, pl.ds(r0 * pitch + m, SUBLANES, stride=pitch), :] = (
                        block[g * SUBLANES:(g + 1) * SUBLANES])
                for r in range(MAX_DIL):
                    piece = stage_ref[col, r * pitch + c * view_rows:r * pitch + (c + 1) * view_rows, :]
                    out_ref[hp, c * view_rows:(c + 1) * view_rows, r * LANES:(r + 1) * LANES] = piece.astype(BF16)

        def rotary(t):
            swapped = jnp.where(first_half,
                                pltpu.roll(t, RET_QK_WIDTH - RET_QK_DIM // 2, axis=1),
                                pltpu.roll(t, RET_QK_DIM // 2, axis=1))
            cos = jnp.concatenate([cos_ref[rows, :]] * (RET_QK_WIDTH // LANES), axis=1)
            sin = jnp.concatenate([sin_ref[rows, :]] * (RET_QK_WIDTH // LANES), axis=1)
            return t * cos + swapped * sin

        qr = rotary(proj[:, base:base + RET_QK_WIDTH])
        kr = rotary(proj[:, base + RET_QK_WIDTH:base + 2 * RET_QK_WIDTH])

        for p in range(N_RET_HEADS // 2):
            cols = slice(p * LANES, (p + 1) * LANES)
            q2 = qr[:, cols]
            k2 = kr[:, cols]
            q_heads = (jnp.where(low_lanes, q2, 0.0), jnp.where(low_lanes, 0.0, q2))
            lhs = jnp.concatenate(q_heads, axis=0).astype(BF16)
            scores = lax.dot_general(lhs, k2.astype(BF16), nt, preferred_element_type=F32)
            kdec = (k2 * kdec_ref[:, cols]).astype(BF16)
            state = state_ref[p]
            state_bf = state.astype(BF16)
            kv = []
            for hh in range(2):
                head = 2 * p + hh
                v_h = proj[:, vbase + head * RET_V_DIM:vbase + (head + 1) * RET_V_DIM].astype(BF16)
                s_h = (scores[hh * RET_CHUNK:(hh + 1) * RET_CHUNK] * dmat_ref[head]).astype(BF16)
                inner = jnp.dot(s_h, v_h, preferred_element_type=F32)
                q_dec = (q_heads[hh] * qdec_ref[:, cols]).astype(BF16)
                cross = jnp.dot(q_dec, state_bf, preferred_element_type=F32)
                o = inner + cross
                o = o * lax.rsqrt(jnp.mean(o * o, axis=-1, keepdims=True) + NORM_EPS)
                g = proj[:, gbase + head * RET_V_DIM:gbase + (head + 1) * RET_V_DIM]
                o = o * (g * (1.0 / (1.0 + jnp.exp(-g))))
                ret_ref[rows, head * RET_V_DIM:(head + 1) * RET_V_DIM] = o.astype(BF16)
                kv.append(lax.dot_general(kdec, v_h, tn, preferred_element_type=F32))
            state_ref[p] = sdec_ref[p] * state + jnp.where(low_rows, kv[0], kv[1])


def _retention_constants(seq):
    freqs = ROPE_BASE ** (-np.arange(0, RET_QK_DIM, 2, dtype=np.float64) / RET_QK_DIM)
    ang = np.arange(seq, dtype=np.float64)[:, None] * freqs[None]
    cos_h = np.concatenate([np.cos(ang), np.cos(ang)], axis=1)
    sin_h = np.concatenate([-np.sin(ang), np.sin(ang)], axis=1)
    cos_t = np.tile(cos_h, (1, LANES // RET_QK_DIM)).astype(np.float32)
    sin_t = np.tile(sin_h, (1, LANES // RET_QK_DIM)).astype(np.float32)
    log_gamma = np.log1p(-np.exp2(-5.0 - np.arange(N_RET_HEADS, dtype=np.float64)))
    idx = np.arange(RET_CHUNK, dtype=np.float64)
    rel = idx[:, None] - idx[None, :]
    dmat = np.where(rel >= 0, np.exp(log_gamma[:, None, None] * np.maximum(rel, 0.0)), 0.0)
    qdec = np.exp(log_gamma[None, :] * (idx + 1.0)[:, None])
    kdec = np.exp(log_gamma[None, :] * (RET_CHUNK - 1 - idx)[:, None])
    qdec = np.repeat(qdec, RET_QK_DIM, axis=1)
    kdec = np.repeat(kdec, RET_QK_DIM, axis=1)
    sdec = np.repeat(np.exp(log_gamma * RET_CHUNK), RET_QK_DIM).reshape(N_RET_HEADS // 2, LANES, 1)
    sdec = np.broadcast_to(sdec, (N_RET_HEADS // 2, LANES, LANES))
    f = lambda a: jnp.asarray(np.ascontiguousarray(a), dtype=F32)
    return f(cos_t), f(sin_t), f(dmat), f(qdec), f(kdec), f(sdec)


def _inproj_retention(x, gain, w_in_bf):
    b, s, d = x.shape
    tm = INPROJ_TILE
    in_width = w_in_bf.shape[1]
    cos_t, sin_t, dmat, qdec, kdec, sdec = _retention_constants(s)
    const = lambda shape: pl.BlockSpec(shape, lambda bi, i: (0,) * len(shape))
    pair_spec = pl.BlockSpec((None, N_PAIRS, tm // MAX_DIL, MAX_DIL * LANES), lambda bi, i: (bi, 0, i, 0))
    pair_shape = jax.ShapeDtypeStruct((b, N_PAIRS, s // MAX_DIL, MAX_DIL * LANES), BF16)
    return pl.pallas_call(
        _inproj_retention_kernel,
        grid=(b, s // tm),
        in_specs=[
            pl.BlockSpec((None, tm, d), lambda bi, i: (bi, i, 0)),
            const((1, d)),
            const((d, in_width)),
            pl.BlockSpec((tm, LANES), lambda bi, i: (i, 0)),
            pl.BlockSpec((tm, LANES), lambda bi, i: (i, 0)),
            const((N_RET_HEADS, RET_CHUNK, RET_CHUNK)),
            const((RET_CHUNK, RET_QK_WIDTH)),
            const((RET_CHUNK, RET_QK_WIDTH)),
            const((N_RET_HEADS // 2, LANES, LANES)),
        ],
        out_specs=[pair_spec, pair_spec, pair_spec,
                   pl.BlockSpec((None, tm, RET_V_WIDTH), lambda bi, i: (bi, i, 0))],
        out_shape=[pair_shape, pair_shape, pair_shape, jax.ShapeDtypeStruct((b, s, RET_V_WIDTH), BF16)],
        scratch_shapes=[pltpu.VMEM((N_RET_HEADS // 2, LANES, LANES), F32),
                        pltpu.VMEM((3 * N_PAIRS, MAX_DIL * (tm // MAX_DIL + SUBLANES), LANES), F32)],
        compiler_params=pltpu.CompilerParams(dimension_semantics=("arbitrary", "arbitrary"),
                                             vmem_limit_bytes=VMEM_LIMIT_BYTES),
        name="inproj_retention",
    )(x, gain, w_in_bf, cos_t, sin_t, dmat, qdec, kdec, sdec)


def _attention_biases():
    q = np.arange(ATTN_ROWS)[None, :]
    k = np.arange(2 * ATTN_ROWS)[:, None]
    out = []
    out += [(ATTN_ROWS + q - k, k >= ATTN_ROWS)]
    out += [(4 * (32 + q % 32 - k % 64) + q // 32 - k // 64, k % 64 >= 32)]
    out += [(16 * (8 + q % 8 - k % 16) + q // 8 - k // 16, k % 16 >= 8)]
    biases = []
    for dist, in_seq in out:
        band = (dist >= 0) & (dist <= ATTN_STEPS)
        biases.append(np.where(band, 0.0, MASK_VALUE))
        biases.append(np.where(band & in_seq, 0.0, MASK_VALUE))
    return jnp.asarray(np.stack(biases), dtype=BF16)


def _attention_kernel(bias_ref, q_ref, k_ref, v_ref, o_ref, *scratch):
    first_tile = pl.program_id(2) == 0
    for hs in range(ATTN_PAIRS):
        _attention_pair(first_tile, bias_ref, q_ref.at[hs], k_ref.at[hs], v_ref.at[hs], o_ref.at[hs],
                        *(ref.at[hs] for ref in scratch))


def _attention_pair(first_tile, bias_ref, q_ref, k_ref, v_ref, o_ref, q0_s, q1_s, k_s, v_s, num_s, den_s, max_s,
                    qb_s, kb_s, vb_s):
    rows = ATTN_ROWS

    @pl.when(first_tile)
    def _():
        k_s[:, 0:rows, :] = jnp.zeros((MAX_DIL, rows, LANES), F32)
        v_s[:, 0:rows, :] = jnp.zeros((MAX_DIL, rows, LANES), F32)
        kb_s[...] = jnp.zeros_like(kb_s)
        vb_s[...] = jnp.zeros_like(vb_s)

    lane = lax.broadcasted_iota(jnp.int32, (1, LANES), 1)
    low = lane < ATTN_HEAD_DIM
    for r in range(MAX_DIL):
        cols = slice(r * LANES, (r + 1) * LANES)
        qf = q_ref[:, cols].astype(F32)
        for head, masked in enumerate((jnp.where(low, qf, 0.0), jnp.where(low, 0.0, qf))):
            (q0_s, q1_s)[head][r] = masked
            qb_s[head, r] = masked.astype(BF16)
        k_s[r, rows:2 * rows, :] = k_ref[:, cols].astype(F32)
        v_s[r, rows:2 * rows, :] = v_ref[:, cols].astype(F32)

    ones = jnp.ones((2 * rows, LANES), BF16)
    pick = (lax.broadcasted_iota(jnp.int32, (2 * rows, LANES), 0) % rows
            == lax.broadcasted_iota(jnp.int32, (2 * rows, LANES), 1))
    pick = jnp.where(pick, 1.0, 0.0).astype(BF16)
    nt = (((1,), (1,)), ((), ()))

    def attend(q2, k2, v2, bias):
        s = lax.dot_general(jnp.concatenate([q2, pick], axis=1), jnp.concatenate([k2, bias], axis=1), nt,
                            preferred_element_type=F32)
        m = jnp.max(s, axis=-1, keepdims=True)
        p = jnp.exp2(s - m).astype(BF16)
        pv = jnp.dot(p, jnp.concatenate([v2, ones], axis=1), preferred_element_type=F32)
        return (jnp.where(low, pv[:rows, :LANES], pv[rows:, :LANES]),
                jnp.where(low, pv[:rows, LANES:], pv[rows:, LANES:]),
                jnp.where(low, m[:rows], m[rows:]))

    def variant(cond):
        return jnp.where(cond, 1, 0)

    def rows_of(pieces, dtype=None):
        stacked = jnp.concatenate(pieces, axis=0)
        return stacked if dtype is None else stacked.astype(dtype)

    def window(hist_s, cur_ref, res, lo, hi):
        cols = slice(res * LANES, (res + 1) * LANES)
        parts = []
        if lo < rows:
            parts.append(hist_s[res, lo:min(hi, rows), :])
        if hi > rows:
            parts.append(cur_ref[max(lo, rows) - rows:hi - rows, cols])
        return parts

    def store(slot, pieces, results):
        for res, rows_dst, rows_src in pieces:
            for ref, val in zip((num_s, den_s, max_s), results):
                ref[slot, res, rows_dst, :] = val[rows_src]

    for c in range(4):
        for j in range(4):
            qrow, klo, khi = slice(32 * j, 32 * j + 32), 96 + 32 * j, 160 + 32 * j
            res = attend(rows_of([qb_s[h, 4 * a + c, qrow, :] for h in range(HEAD_PAIR) for a in range(4)]),
                         rows_of([p for a in range(4) for p in window(kb_s, k_ref, 4 * a + c, klo, khi)]),
                         rows_of([p for a in range(4) for p in window(vb_s, v_ref, 4 * a + c, klo, khi)]),
                         bias_ref[2 + variant(first_tile)] if j == 0 else bias_ref[2])
            store(0, [(4 * a + c, qrow, slice(32 * a, 32 * a + 32)) for a in range(4)], res)

    for j in range(MAX_DIL):
        qrow, krow = slice(8 * j, 8 * j + 8), slice(120 + 8 * j, 136 + 8 * j)
        res = attend(rows_of([q_s[r, qrow, :] for q_s in (q0_s, q1_s) for r in range(MAX_DIL)], BF16),
                     rows_of([k_s[r, krow, :] for r in range(MAX_DIL)], BF16),
                     rows_of([v_s[r, krow, :] for r in range(MAX_DIL)], BF16),
                     bias_ref[4 + variant(first_tile)] if j == 0 else bias_ref[4])
        store(1, [(r, qrow, slice(8 * r, 8 * r + 8)) for r in range(MAX_DIL)], res)

    for r in range(MAX_DIL):
        n0, d0, m0 = attend(rows_of([qb_s[0, r], qb_s[1, r]]), rows_of(window(kb_s, k_ref, r, 0, 2 * rows)),
                            rows_of(window(vb_s, v_ref, r, 0, 2 * rows)), bias_ref[variant(first_tile)])
        m1, m2 = max_s[0, r], max_s[1, r]
        top = jnp.maximum(jnp.maximum(m0, m1), m2)
        e0, e1, e2 = jnp.exp2(m0 - top), jnp.exp2(m1 - top), jnp.exp2(m2 - top)
        num = e0 * n0 + e1 * num_s[0, r] + e2 * num_s[1, r]
        den = e0 * d0 + e1 * den_s[0, r] + e2 * den_s[1, r]
        o_ref[:, r * LANES:(r + 1) * LANES] = (num * (1.0 / den)).astype(BF16)
        k_s[r, 0:rows, :] = k_s[r, rows:2 * rows, :]
        v_s[r, 0:rows, :] = v_s[r, rows:2 * rows, :]
        kb_s[r] = k_ref[:, r * LANES:(r + 1) * LANES]
        vb_s[r] = v_ref[:, r * LANES:(r + 1) * LANES]


def _dilated_attention(qa, ka, va):
    b, n_pairs, view_len, width = qa.shape
    tile = pl.BlockSpec((None, ATTN_PAIRS, ATTN_ROWS, width), lambda bi, hp, i: (bi, hp, i, 0))
    biases = _attention_biases()
    scratch = lambda n: pltpu.VMEM((ATTN_PAIRS, MAX_DIL, n, LANES), F32)
    return pl.pallas_call(
        _attention_kernel,
        grid=(b, n_pairs // ATTN_PAIRS, view_len // ATTN_ROWS),
        in_specs=[pl.BlockSpec(biases.shape, lambda bi, hp, i: (0, 0, 0)), tile, tile, tile],
        out_specs=tile,
        out_shape=jax.ShapeDtypeStruct(qa.shape, BF16),
        scratch_shapes=[scratch(ATTN_ROWS), scratch(ATTN_ROWS), scratch(2 * ATTN_ROWS), scratch(2 * ATTN_ROWS)]
        + [pltpu.VMEM((ATTN_PAIRS, len(DILATIONS) - 1, MAX_DIL, ATTN_ROWS, LANES), F32)] * 3
        + [pltpu.VMEM((ATTN_PAIRS, HEAD_PAIR, MAX_DIL, ATTN_ROWS, LANES), BF16)]
        + [pltpu.VMEM((ATTN_PAIRS, MAX_DIL, ATTN_ROWS, LANES), BF16)] * 2,
        compiler_params=pltpu.CompilerParams(dimension_semantics=("arbitrary", "arbitrary", "arbitrary"),
                                             vmem_limit_bytes=VMEM_LIMIT_BYTES),
        name="dilated_attention",
    )(biases, qa, ka, va)


def _outproj_ffn_kernel(x_ref, attn_ref, ret_ref, gpost_ref, gpre_ref, gffn_ref, wout_ref, wup_ref, cw_ref, cb_ref,
                        wdown_ref, o_ref, act_s, halo_s, stage_s):
    tm = x_ref.shape[0]
    d_ff = wdown_ref.shape[0]

    @pl.when(pl.program_id(1) == 0)
    def _():
        halo_s[...] = jnp.zeros_like(halo_s)

    first_rows = lax.broadcasted_iota(jnp.int32, (SUBLANES, 1), 0)
    hm = tm // FF_STREAMS
    vm = hm // MAX_DIL
    x1, hn = [], []
    for st in range(FF_STREAMS):
        rows = slice(st * hm, (st + 1) * hm)
        for hp in range(N_PAIRS):
            for r in range(MAX_DIL):
                piece = attn_ref[hp, st * vm:(st + 1) * vm, r * LANES:(r + 1) * LANES].astype(F32)
                stage_s[hp, pl.ds(st * hm + r, vm, stride=MAX_DIL), :] = piece
        mixed = jnp.concatenate([stage_s[hp, rows, :].astype(BF16) for hp in range(N_PAIRS)] + [ret_ref[rows, :]],
                                axis=-1)
        mix = _dot_resident(mixed, wout_ref)
        x1.append(x_ref[rows, :] + _rms(mix, gpost_ref[...]))
        hn.append(_rms(x1[st], gpre_ref[...]).astype(BF16))

    def conv(cols, u):
        halo = halo_s[:, cols]
        halo_s[:, cols] = u[hm - SUBLANES:hm]
        taps = []
        for shift in (1, 2):
            rolled = pltpu.roll(u, shift, axis=0)
            head = jnp.where(first_rows < shift, pltpu.roll(halo, shift, axis=0), rolled[0:SUBLANES])
            taps.append(jnp.concatenate([head, rolled[SUBLANES:]], axis=0))
        return cb_ref[:, cols] + cw_ref[2:3, cols] * u + cw_ref[1:2, cols] * taps[0] + cw_ref[0:1, cols] * taps[1]

    for c in range(d_ff // FF_CHUNK):
        gcols = slice(c * FF_CHUNK, (c + 1) * FF_CHUNK)
        ucols = slice(d_ff + c * FF_CHUNK, d_ff + (c + 1) * FF_CHUNK)
        for st in range(FF_STREAMS):
            gate = conv(gcols, jnp.dot(hn[st], wup_ref[:, gcols], preferred_element_type=F32))
            half_up = conv(ucols, jnp.dot(hn[st], wup_ref[:, ucols], preferred_element_type=F32))
            inner = gate * (GELU_C1 + GELU_C3 * (gate * gate))
            act_s[st * hm:(st + 1) * hm, gcols] = (gate * (1.0 + jnp.tanh(inner)) * half_up).astype(BF16)

    for st in range(FF_STREAMS):
        rows = slice(st * hm, (st + 1) * hm)
        ffn = _dot_resident(act_s[rows, :], wdown_ref)
        o_ref[rows, :] = x1[st] + _rms(ffn, gffn_ref[...])


def _outproj_ffn(x, attn, ret, g_post, g_pre, g_ffn, w_out_bf, w_up_bf, conv_w, conv_b, w_down_bf):
    b, s, d = x.shape
    tm = TOKEN_TILE
    d_ff = w_down_bf.shape[0]
    const = lambda shape: pl.BlockSpec(shape, lambda bi, i: (0,) * len(shape), pipeline_mode=pl.Buffered(1))
    return pl.pallas_call(
        _outproj_ffn_kernel,
        grid=(b, s // tm),
        in_specs=[
            pl.BlockSpec((None, tm, d), lambda bi, i: (bi, i, 0)),
            pl.BlockSpec((None, N_PAIRS, tm // MAX_DIL, MAX_DIL * LANES), lambda bi, i: (bi, 0, i, 0)),
            pl.BlockSpec((None, tm, RET_V_WIDTH), lambda bi, i: (bi, i, 0)),
            const((1, d)), const((1, d)), const((1, d)),
            const(w_out_bf.shape), const(w_up_bf.shape), const(conv_w.shape), const(conv_b.shape),
            const(w_down_bf.shape),
        ],
        out_specs=pl.BlockSpec((None, tm, d), lambda bi, i: (bi, i, 0)),
        out_shape=jax.ShapeDtypeStruct((b, s, d), F32),
        scratch_shapes=[pltpu.VMEM((tm, d_ff), BF16),
                        pltpu.VMEM((SUBLANES, 2 * d_ff), F32),
                        pltpu.VMEM((N_PAIRS, tm, LANES), F32)],
        compiler_params=pltpu.CompilerParams(dimension_semantics=("arbitrary", "arbitrary"),
                                             vmem_limit_bytes=VMEM_LIMIT_BYTES),
        name="outproj_ffn",
    )(x, attn, ret, g_post, g_pre, g_ffn, w_out_bf, w_up_bf, conv_w, conv_b, w_down_bf)


def _layer(x, g_mix_pre, g_mix_post, w_in, w_out, g_ffn_pre, g_ffn_post, w_up, conv_w, conv_b, w_down):
    d = x.shape[-1]
    d_ff = w_down.shape[0]
    col_scale = np.ones((w_in.shape[1],), np.float32)
    col_scale[:ATTN_WIDTH] = ATTN_HEAD_DIM ** -0.5 * np.log2(np.e)
    kr0 = 3 * ATTN_WIDTH + RET_QK_WIDTH
    col_scale[kr0:kr0 + RET_QK_WIDTH] = RET_QK_DIM ** -0.5
    w_in_bf = _pad_columns((w_in * col_scale).astype(BF16))
    qa, ka, va, ret = _inproj_retention(x, g_mix_pre.reshape(1, d), w_in_bf)
    attn = _dilated_attention(qa, ka, va)
    up_half = np.concatenate([np.ones((d_ff,), np.float32), np.full((d_ff,), 0.5, np.float32)])
    return _outproj_ffn(x, attn, ret, g_mix_post.reshape(1, d), g_ffn_pre.reshape(1, d), g_ffn_post.reshape(1, d),
                        _pad_columns(w_out.astype(BF16)), _pad_columns(w_up.astype(BF16)), conv_w * up_half,
                        conv_b.reshape(1, -1) * up_half, _pad_columns(w_down.astype(BF16)))


def kernel(x, mix_pre_gain, mix_post_gain, w_in, w_out, ffn_pre_gain, ffn_post_gain, w_up, conv_w, conv_b, w_down):
    for layer in range(mix_pre_gain.shape[0]):
        x = _layer(x, mix_pre_gain[layer], mix_post_gain[layer], w_in[layer], w_out[layer], ffn_pre_gain[layer],
                   ffn_post_gain[layer], w_up[layer], conv_w[layer], conv_b[layer], w_down[layer])
    return x
```

```python
import numpy as np
import jax
import jax.numpy as jnp
from jax import lax
from jax.experimental import pallas as pl
from jax.experimental.pallas import tpu as pltpu

F32 = jnp.float32
BF16 = jnp.bfloat16

ATTN_HEAD_DIM = 64
N_ATTN_HEADS = 8
ATTN_WIDTH = N_ATTN_HEADS * ATTN_HEAD_DIM
ATTN_STEPS = 128
DILATIONS = (1, 4, 16)
PATTERN_ORDER = (16, 4, 1)
RET_V_DIM = 128
N_RET_HEADS = 4
RET_QK_DIM = 64
RET_QK_WIDTH = N_RET_HEADS * RET_QK_DIM
RET_V_WIDTH = N_RET_HEADS * RET_V_DIM
ROPE_BASE = 10000.0
CONV_WIDTH = 3
NORM_EPS = 1e-6
MASK_VALUE = -1e30

LANES = 128
SUBLANES = 8
BF16_ROWS = 2 * SUBLANES
HEAD_PAIR = LANES // ATTN_HEAD_DIM
N_PAIRS = N_ATTN_HEADS // HEAD_PAIR
MAX_DIL = max(DILATIONS)
ATTN_ROWS = 128
TOKEN_TILE = 1024
INPROJ_TILE = 1024
RET_CHUNK = 256
ATTN_PAIRS = 2
FF_CHUNK = 256
FF_STREAMS = 4
GELU_C1 = float(np.sqrt(2.0 / np.pi).astype(np.float32))
GELU_C3 = float(np.float32(GELU_C1 * 0.044715))
VMEM_LIMIT_BYTES = 56 * 1024 * 1024


def _pad_columns(w):
    return jnp.concatenate([w, jnp.zeros((w.shape[0], LANES), w.dtype)], axis=1)


def _dot_resident(lhs, w_ref):
    return jnp.dot(lhs, w_ref[:, :w_ref.shape[1] - LANES], preferred_element_type=F32)


def _rms(x, gain):
    return x * lax.rsqrt(jnp.mean(x * x, axis=-1, keepdims=True) + NORM_EPS) * gain


def _inproj_retention_kernel(x_ref, gain_ref, w_ref, cos_ref, sin_ref, dmat_ref, qdec_ref, kdec_ref, sdec_ref,
                             qa_ref, ka_ref, va_ref, ret_ref, state_ref, stage_ref):
    @pl.when(pl.program_id(1) == 0)
    def _():
        state_ref[...] = jnp.zeros_like(state_ref)

    base = 3 * ATTN_WIDTH
    vbase = base + 2 * RET_QK_WIDTH
    gbase = vbase + RET_V_WIDTH
    lane = lax.broadcasted_iota(jnp.int32, (1, RET_QK_WIDTH), 1)
    first_half = (lane % RET_QK_DIM) < (RET_QK_DIM // 2)
    lane128 = lax.broadcasted_iota(jnp.int32, (1, LANES), 1)
    low_lanes = lane128 < RET_QK_DIM
    row128 = lax.broadcasted_iota(jnp.int32, (LANES, 1), 0)
    low_rows = row128 < RET_QK_DIM
    nt = (((1,), (1,)), ((), ()))
    tn = (((0,), (0,)), ((), ()))
    view_rows = RET_CHUNK // MAX_DIL
    pitch = stage_ref.shape[1] // MAX_DIL

    for c in range(x_ref.shape[0] // RET_CHUNK):
        rows = slice(c * RET_CHUNK, (c + 1) * RET_CHUNK)
        h = _rms(x_ref[rows, :], gain_ref[...]).astype(BF16)
        proj = _dot_resident(h, w_ref)

        for t, out_ref in enumerate((qa_ref, ka_ref, va_ref)):
            for hp in range(N_PAIRS):
                col = t * N_PAIRS + hp
                block = proj[:, col * LANES:(col + 1) * LANES]
                for g in range(RET_CHUNK // SUBLANES):
                    m = c * view_rows + g * SUBLANES // MAX_DIL
                    r0 = g * SUBLANES % MAX_DIL
                    stage_ref[col, pl.ds(r0 * pitch + m, SUBLANES, stride=pitch), :] = (
                        block[g * SUBLANES:(g + 1) * SUBLANES])
                for r in range(MAX_DIL):
                    piece = stage_ref[col, r * pitch + c * view_rows:r * pitch + (c + 1) * view_rows, :]
                    out_ref[hp, c * view_rows:(c + 1) * view_rows, r * LANES:(r + 1) * LANES] = piece.astype(BF16)

        def rotary(t):
            swapped = jnp.where(first_half,
                                pltpu.roll(t, RET_QK_WIDTH - RET_QK_DIM // 2, axis=1),
                                pltpu.roll(t, RET_QK_DIM // 2, axis=1))
            cos = jnp.concatenate([cos_ref[rows, :]] * (RET_QK_WIDTH // LANES), axis=1)
            sin = jnp.concatenate([sin_ref[rows, :]] * (RET_QK_WIDTH // LANES), axis=1)
            return t * cos + swapped * sin

        qr = rotary(proj[:, base:base + RET_QK_WIDTH])
        kr = rotary(proj[:, base + RET_QK_WIDTH:base + 2 * RET_QK_WIDTH])

        for p in range(N_RET_HEADS // 2):
            cols = slice(p * LANES, (p + 1) * LANES)
            q2 = qr[:, cols]
            k2 = kr[:, cols]
            q_heads = (jnp.where(low_lanes, q2, 0.0), jnp.where(low_lanes, 0.0, q2))
            lhs = jnp.concatenate(q_heads, axis=0).astype(BF16)
            scores = lax.dot_general(lhs, k2.astype(BF16), nt, preferred_element_type=F32)
            kdec = (k2 * kdec_ref[:, cols]).astype(BF16)
            state = state_ref[p]
            state_bf = state.astype(BF16)
            kv = []
            for hh in range(2):
                head = 2 * p + hh
                v_h = proj[:, vbase + head * RET_V_DIM:vbase + (head + 1) * RET_V_DIM].astype(BF16)
                s_h = (scores[hh * RET_CHUNK:(hh + 1) * RET_CHUNK] * dmat_ref[head]).astype(BF16)
                inner = jnp.dot(s_h, v_h, preferred_element_type=F32)
                q_dec = (q_heads[hh] * qdec_ref[:, cols]).astype(BF16)
                cross = jnp.dot(q_dec, state_bf, preferred_element_type=F32)
                o = inner + cross
                o = o * lax.rsqrt(jnp.mean(o * o, axis=-1, keepdims=True) + NORM_EPS)
                g = proj[:, gbase + head * RET_V_DIM:gbase + (head + 1) * RET_V_DIM]
                o = o * (g * (1.0 / (1.0 + jnp.exp(-g))))
                ret_ref[rows, head * RET_V_DIM:(head + 1) * RET_V_DIM] = o.astype(BF16)
                kv.append(lax.dot_general(kdec, v_h, tn, preferred_element_type=F32))
            state_ref[p] = sdec_ref[p] * state + jnp.where(low_rows, kv[0], kv[1])


def _retention_constants(seq):
    freqs = ROPE_BASE ** (-np.arange(0, RET_QK_DIM, 2, dtype=np.float64) / RET_QK_DIM)
    ang = np.arange(seq, dtype=np.float64)[:, None] * freqs[None]
    cos_h = np.concatenate([np.cos(ang), np.cos(ang)], axis=1)
    sin_h = np.concatenate([-np.sin(ang), np.sin(ang)], axis=1)
    cos_t = np.tile(cos_h, (1, LANES // RET_QK_DIM)).astype(np.float32)
    sin_t = np.tile(sin_h, (1, LANES // RET_QK_DIM)).astype(np.float32)
    log_gamma = np.log1p(-np.exp2(-5.0 - np.arange(N_RET_HEADS, dtype=np.float64)))
    idx = np.arange(RET_CHUNK, dtype=np.float64)
    rel = idx[:, None] - idx[None, :]
    dmat = np.where(rel >= 0, np.exp(log_gamma[:, None, None] * np.maximum(rel, 0.0)), 0.0)
    qdec = np.exp(log_gamma[None, :] * (idx + 1.0)[:, None])
    kdec = np.exp(log_gamma[None, :] * (RET_CHUNK - 1 - idx)[:, None])
    qdec = np.repeat(qdec, RET_QK_DIM, axis=1)
    kdec = np.repeat(kdec, RET_QK_DIM, axis=1)
    sdec = np.repeat(np.exp(log_gamma * RET_CHUNK), RET_QK_DIM).reshape(N_RET_HEADS // 2, LANES, 1)
    sdec = np.broadcast_to(sdec, (N_RET_HEADS // 2, LANES, LANES))
    f = lambda a: jnp.asarray(np.ascontiguousarray(a), dtype=F32)
    return f(cos_t), f(sin_t), f(dmat), f(qdec), f(kdec), f(sdec)


def _inproj_retention(x, gain, w_in_bf):
    b, s, d = x.shape
    tm = INPROJ_TILE
    in_width = w_in_bf.shape[1]
    cos_t, sin_t, dmat, qdec, kdec, sdec = _retention_constants(s)
    const = lambda shape: pl.BlockSpec(shape, lambda bi, i: (0,) * len(shape))
    pair_spec = pl.BlockSpec((None, N_PAIRS, tm // MAX_DIL, MAX_DIL * LANES), lambda bi, i: (bi, 0, i, 0))
    pair_shape = jax.ShapeDtypeStruct((b, N_PAIRS, s // MAX_DIL, MAX_DIL * LANES), BF16)
    return pl.pallas_call(
        _inproj_retention_kernel,
        grid=(b, s // tm),
        in_specs=[
            pl.BlockSpec((None, tm, d), lambda bi, i: (bi, i, 0)),
            const((1, d)),
            const((d, in_width)),
            pl.BlockSpec((tm, LANES), lambda bi, i: (i, 0)),
            pl.BlockSpec((tm, LANES), lambda bi, i: (i, 0)),
            const((N_RET_HEADS, RET_CHUNK, RET_CHUNK)),
            const((RET_CHUNK, RET_QK_WIDTH)),
            const((RET_CHUNK, RET_QK_WIDTH)),
            const((N_RET_HEADS // 2, LANES, LANES)),
        ],
        out_specs=[pair_spec, pair_spec, pair_spec,
                   pl.BlockSpec((None, tm, RET_V_WIDTH), lambda bi, i: (bi, i, 0))],
        out_shape=[pair_shape, pair_shape, pair_shape, jax.ShapeDtypeStruct((b, s, RET_V_WIDTH), BF16)],
        scratch_shapes=[pltpu.VMEM((N_RET_HEADS // 2, LANES, LANES), F32),
                        pltpu.VMEM((3 * N_PAIRS, MAX_DIL * (tm // MAX_DIL + SUBLANES), LANES), F32)],
        compiler_params=pltpu.CompilerParams(dimension_semantics=("arbitrary", "arbitrary"),
                                             vmem_limit_bytes=VMEM_LIMIT_BYTES),
        name="inproj_retention",
    )(x, gain, w_in_bf, cos_t, sin_t, dmat, qdec, kdec, sdec)


def _attention_biases():
    q = np.arange(ATTN_ROWS)[None, :]
    k = np.arange(2 * ATTN_ROWS)[:, None]
    out = []
    for d in PATTERN_ORDER:
        g = MAX_DIL // d
        qn = ATTN_ROWS // g
        dist = g * (qn + q % qn - k % (2 * qn)) + q // qn - k // (2 * qn)
        out += [(dist, k % (2 * qn) >= qn)]
    biases = []
    for dist, in_seq in out:
        band = (dist >= 0) & (dist <= ATTN_STEPS)
        biases.append(np.where(band, 0.0, MASK_VALUE))
        biases.append(np.where(band & in_seq, 0.0, MASK_VALUE))
    return jnp.asarray(np.stack(biases), dtype=BF16)


def _attention_kernel(bias_ref, q_ref, k_ref, v_ref, o_ref, *scratch):
    first_tile = pl.program_id(2) == 0
    for hs in range(ATTN_PAIRS):
        _attention_pair(first_tile, bias_ref, q_ref.at[hs], k_ref.at[hs], v_ref.at[hs], o_ref.at[hs],
                        *(ref.at[hs] for ref in scratch))


def _attention_pair(first_tile, bias_ref, q_ref, k_ref, v_ref, o_ref, q0_s, q1_s, k_s, v_s, num_s, den_s, max_s,
                    qb_s, kb_s, vb_s):
    rows = ATTN_ROWS

    @pl.when(first_tile)
    def _():
        k_s[:, 0:rows, :] = jnp.zeros((MAX_DIL, rows, LANES), F32)
        v_s[:, 0:rows, :] = jnp.zeros((MAX_DIL, rows, LANES), F32)
        kb_s[...] = jnp.zeros_like(kb_s)
        vb_s[...] = jnp.zeros_like(vb_s)

    lane = lax.broadcasted_iota(jnp.int32, (1, LANES), 1)
    low = lane < ATTN_HEAD_DIM
    for r in range(MAX_DIL):
        cols = slice(r * LANES, (r + 1) * LANES)
        qf = q_ref[:, cols].astype(F32)
        for head, masked in enumerate((jnp.where(low, qf, 0.0), jnp.where(low, 0.0, qf))):
            (q0_s, q1_s)[head][r] = masked
            qb_s[head, r] = masked.astype(BF16)
        k_s[r, rows:2 * rows, :] = k_ref[:, cols].astype(F32)
        v_s[r, rows:2 * rows, :] = v_ref[:, cols].astype(F32)

    ones = jnp.ones((2 * rows, LANES), BF16)
    pick = (lax.broadcasted_iota(jnp.int32, (2 * rows, LANES), 0) % rows
            == lax.broadcasted_iota(jnp.int32, (2 * rows, LANES), 1))
    pick = jnp.where(pick, 1.0, 0.0).astype(BF16)
    nt = (((1,), (1,)), ((), ()))

    def attend(q2, k2, v2, bias):
        s = lax.dot_general(jnp.concatenate([q2, pick], axis=1), jnp.concatenate([k2, bias], axis=1), nt,
                            preferred_element_type=F32)
        m = jnp.max(s, axis=-1, keepdims=True)
        p = jnp.exp2(s - m).astype(BF16)
        pv = jnp.dot(p, jnp.concatenate([v2, ones], axis=1), preferred_element_type=F32)
        return (jnp.where(low, pv[:rows, :LANES], pv[rows:, :LANES]),
                jnp.where(low, pv[:rows, LANES:], pv[rows:, LANES:]),
                jnp.where(low, m[:rows], m[rows:]))

    def variant(cond):
        return jnp.where(cond, 1, 0)

    def rows_of(pieces, dtype=None):
        stacked = jnp.concatenate(pieces, axis=0)
        return stacked if dtype is None else stacked.astype(dtype)

    def window(hist_s, cur_ref, res, lo, hi):
        cols = slice(res * LANES, (res + 1) * LANES)
        parts = []
        if lo < rows:
            parts.append(hist_s[res, lo:min(hi, rows), :])
        if hi > rows:
            parts.append(cur_ref[max(lo, rows) - rows:hi - rows, cols])
        return parts

    def store(slot, pieces, results):
        for res, rows_dst, rows_src in pieces:
            for ref, val in zip((num_s, den_s, max_s), results):
                ref[slot, res, rows_dst, :] = val[rows_src]

    def unit(d, c, j):
        g = MAX_DIL // d
        qn = rows // g
        residues = [a * d + c for a in range(g)]
        qrow, klo, khi = slice(qn * j, qn * (j + 1)), rows - qn + qn * j, rows + qn + qn * j
        table = 2 * PATTERN_ORDER.index(d)
        bias = bias_ref[table + variant(first_tile)] if j == 0 else bias_ref[table]
        if qn % BF16_ROWS == 0:
            q2 = rows_of([qb_s[h, res, qrow, :] for h in range(HEAD_PAIR) for res in residues])
            k2 = rows_of([p for res in residues for p in window(kb_s, k_ref, res, klo, khi)])
            v2 = rows_of([p for res in residues for p in window(vb_s, v_ref, res, klo, khi)])
        else:
            q2 = rows_of([q_s[res, qrow, :] for q_s in (q0_s, q1_s) for res in residues], BF16)
            k2 = rows_of([k_s[res, klo:khi, :] for res in residues], BF16)
            v2 = rows_of([v_s[res, klo:khi, :] for res in residues], BF16)
        return attend(q2, k2, v2, bias), [(res, qrow, slice(qn * a, qn * (a + 1))) for a, res in enumerate(residues)]

    for slot, d in enumerate(PATTERN_ORDER[1:]):
        for c in range(d):
            for j in range(MAX_DIL // d):
                results, pieces = unit(d, c, j)
                store(slot, pieces, results)

    for r in range(MAX_DIL):
        (n0, d0, m0), _ = unit(MAX_DIL, r, 0)
        m1, m2 = max_s[0, r], max_s[1, r]
        top = jnp.maximum(jnp.maximum(m0, m1), m2)
        e0, e1, e2 = jnp.exp2(m0 - top), jnp.exp2(m1 - top), jnp.exp2(m2 - top)
        num = e0 * n0 + e1 * num_s[0, r] + e2 * num_s[1, r]
        den = e0 * d0 + e1 * den_s[0, r] + e2 * den_s[1, r]
        o_ref[:, r * LANES:(r + 1) * LANES] = (num * (1.0 / den)).astype(BF16)
        k_s[r, 0:rows, :] = k_s[r, rows:2 * rows, :]
        v_s[r, 0:rows, :] = v_s[r, rows:2 * rows, :]
        kb_s[r] = k_ref[:, r * LANES:(r + 1) * LANES]
        vb_s[r] = v_ref[:, r * LANES:(r + 1) * LANES]


def _dilated_attention(qa, ka, va):
    b, n_pairs, view_len, width = qa.shape
    tile = pl.BlockSpec((None, ATTN_PAIRS, ATTN_ROWS, width), lambda bi, hp, i: (bi, hp, i, 0))
    biases = _attention_biases()
    scratch = lambda n: pltpu.VMEM((ATTN_PAIRS, MAX_DIL, n, LANES), F32)
    return pl.pallas_call(
        _attention_kernel,
        grid=(b, n_pairs // ATTN_PAIRS, view_len // ATTN_ROWS),
        in_specs=[pl.BlockSpec(biases.shape, lambda bi, hp, i: (0, 0, 0)), tile, tile, tile],
        out_specs=tile,
        out_shape=jax.ShapeDtypeStruct(qa.shape, BF16),
        scratch_shapes=[scratch(ATTN_ROWS), scratch(ATTN_ROWS), scratch(2 * ATTN_ROWS), scratch(2 * ATTN_ROWS)]
        + [pltpu.VMEM((ATTN_PAIRS, len(DILATIONS) - 1, MAX_DIL, ATTN_ROWS, LANES), F32)] * 3
        + [pltpu.VMEM((ATTN_PAIRS, HEAD_PAIR, MAX_DIL, ATTN_ROWS, LANES), BF16)]
        + [pltpu.VMEM((ATTN_PAIRS, MAX_DIL, ATTN_ROWS, LANES), BF16)] * 2,
        compiler_params=pltpu.CompilerParams(dimension_semantics=("arbitrary", "arbitrary", "arbitrary"),
                                             vmem_limit_bytes=VMEM_LIMIT_BYTES),
        name="dilated_attention",
    )(biases, qa, ka, va)


def _outproj_ffn_kernel(x_ref, attn_ref, ret_ref, gpost_ref, gpre_ref, gffn_ref, wout_ref, wup_ref, cw_ref, cb_ref,
                        wdown_ref, o_ref, act_s, halo_s, stage_s):
    tm = x_ref.shape[0]
    d_ff = wdown_ref.shape[0]

    @pl.when(pl.program_id(1) == 0)
    def _():
        halo_s[...] = jnp.zeros_like(halo_s)

    first_rows = lax.broadcasted_iota(jnp.int32, (SUBLANES, 1), 0)
    hm = tm // FF_STREAMS
    vm = hm // MAX_DIL
    x1, hn = [], []
    for st in range(FF_STREAMS):
        rows = slice(st * hm, (st + 1) * hm)
        for hp in range(N_PAIRS):
            for r in range(MAX_DIL):
                piece = attn_ref[hp, st * vm:(st + 1) * vm, r * LANES:(r + 1) * LANES].astype(F32)
                stage_s[hp, pl.ds(st * hm + r, vm, stride=MAX_DIL), :] = piece
        mixed = jnp.concatenate([stage_s[hp, rows, :].astype(BF16) for hp in range(N_PAIRS)] + [ret_ref[rows, :]],
                                axis=-1)
        mix = _dot_resident(mixed, wout_ref)
        x1.append(x_ref[rows, :] + _rms(mix, gpost_ref[...]))
        hn.append(_rms(x1[st], gpre_ref[...]).astype(BF16))

    def conv(cols, u):
        halo = halo_s[:, cols]
        halo_s[:, cols] = u[hm - SUBLANES:hm]
        taps = []
        for shift in (1, 2):
            rolled = pltpu.roll(u, shift, axis=0)
            head = jnp.where(first_rows < shift, pltpu.roll(halo, shift, axis=0), rolled[0:SUBLANES])
            taps.append(jnp.concatenate([head, rolled[SUBLANES:]], axis=0))
        return cb_ref[:, cols] + cw_ref[2:3, cols] * u + cw_ref[1:2, cols] * taps[0] + cw_ref[0:1, cols] * taps[1]

    for c in range(d_ff // FF_CHUNK):
        gcols = slice(c * FF_CHUNK, (c + 1) * FF_CHUNK)
        ucols = slice(d_ff + c * FF_CHUNK, d_ff + (c + 1) * FF_CHUNK)
        for st in range(FF_STREAMS):
            gate = conv(gcols, jnp.dot(hn[st], wup_ref[:, gcols], preferred_element_type=F32))
            half_up = conv(ucols, jnp.dot(hn[st], wup_ref[:, ucols], preferred_element_type=F32))
            inner = gate * (GELU_C1 + GELU_C3 * (gate * gate))
            act_s[st * hm:(st + 1) * hm, gcols] = (gate * (1.0 + jnp.tanh(inner)) * half_up).astype(BF16)

    for st in range(FF_STREAMS):
        rows = slice(st * hm, (st + 1) * hm)
        ffn = _dot_resident(act_s[rows, :], wdown_ref)
        o_ref[rows, :] = x1[st] + _rms(ffn, gffn_ref[...])


def _outproj_ffn(x, attn, ret, g_post, g_pre, g_ffn, w_out_bf, w_up_bf, conv_w, conv_b, w_down_bf):
    b, s, d = x.shape
    tm = TOKEN_TILE
    d_ff = w_down_bf.shape[0]
    const = lambda shape: pl.BlockSpec(shape, lambda bi, i: (0,) * len(shape), pipeline_mode=pl.Buffered(1))
    return pl.pallas_call(
        _outproj_ffn_kernel,
        grid=(b, s // tm),
        in_specs=[
            pl.BlockSpec((None, tm, d), lambda bi, i: (bi, i, 0)),
            pl.BlockSpec((None, N_PAIRS, tm // MAX_DIL, MAX_DIL * LANES), lambda bi, i: (bi, 0, i, 0)),
            pl.BlockSpec((None, tm, RET_V_WIDTH), lambda bi, i: (bi, i, 0)),
            const((1, d)), const((1, d)), const((1, d)),
            const(w_out_bf.shape), const(w_up_bf.shape), const(conv_w.shape), const(conv_b.shape),
            const(w_down_bf.shape),
        ],
        out_specs=pl.BlockSpec((None, tm, d), lambda bi, i: (bi, i, 0)),
        out_shape=jax.ShapeDtypeStruct((b, s, d), F32),
        scratch_shapes=[pltpu.VMEM((tm, d_ff), BF16),
                        pltpu.VMEM((SUBLANES, 2 * d_ff), F32),
                        pltpu.VMEM((N_PAIRS, tm, LANES), F32)],
        compiler_params=pltpu.CompilerParams(dimension_semantics=("arbitrary", "arbitrary"),
                                             vmem_limit_bytes=VMEM_LIMIT_BYTES),
        name="outproj_ffn",
    )(x, attn, ret, g_post, g_pre, g_ffn, w_out_bf, w_up_bf, conv_w, conv_b, w_down_bf)


def _layer(x, g_mix_pre, g_mix_post, w_in, w_out, g_ffn_pre, g_ffn_post, w_up, conv_w, conv_b, w_down):
    d = x.shape[-1]
    d_ff = w_down.shape[0]
    col_scale = np.ones((w_in.shape[1],), np.float32)
    col_scale[:ATTN_WIDTH] = ATTN_HEAD_DIM ** -0.5 * np.log2(np.e)
    kr0 = 3 * ATTN_WIDTH + RET_QK_WIDTH
    col_scale[kr0:kr0 + RET_QK_WIDTH] = RET_QK_DIM ** -0.5
    w_in_bf = _pad_columns((w_in * col_scale).astype(BF16))
    qa, ka, va, ret = _inproj_retention(x, g_mix_pre.reshape(1, d), w_in_bf)
    attn = _dilated_attention(qa, ka, va)
    up_half = np.concatenate([np.ones((d_ff,), np.float32), np.full((d_ff,), 0.5, np.float32)])
    return _outproj_ffn(x, attn, ret, g_mix_post.reshape(1, d), g_ffn_pre.reshape(1, d), g_ffn_post.reshape(1, d),
                        _pad_columns(w_out.astype(BF16)), _pad_columns(w_up.astype(BF16)), conv_w * up_half,
                        conv_b.reshape(1, -1) * up_half, _pad_columns(w_down.astype(BF16)))


def kernel(x, mix_pre_gain, mix_post_gain, w_in, w_out, ffn_pre_gain, ffn_post_gain, w_up, conv_w, conv_b, w_down):
    for layer in range(mix_pre_gain.shape[0]):
        x = _layer(x, mix_pre_gain[layer], mix_post_gain[layer], w_in[layer], w_out[layer], ffn_pre_gain[layer],
                   ffn_post_gain[layer], w_up[layer], conv_w[layer], conv_b[layer], w_down[layer])
    return x
```

```python
import numpy as np
import jax
import jax.numpy as jnp
from jax import lax
from jax.experimental import pallas as pl
from jax.experimental.pallas import tpu as pltpu

F32 = jnp.float32
BF16 = jnp.bfloat16

ATTN_HEAD_DIM = 64
N_ATTN_HEADS = 8
ATTN_WIDTH = N_ATTN_HEADS * ATTN_HEAD_DIM
ATTN_STEPS = 128
DILATIONS = (1, 4, 16)
PATTERN_ORDER = (16, 4, 1)
RET_V_DIM = 128
N_RET_HEADS = 4
RET_QK_DIM = 64
RET_QK_WIDTH = N_RET_HEADS * RET_QK_DIM
RET_V_WIDTH = N_RET_HEADS * RET_V_DIM
ROPE_BASE = 10000.0
CONV_WIDTH = 3
NORM_EPS = 1e-6
MASK_VALUE = -1e30

LANES = 128
SUBLANES = 8
BF16_ROWS = 2 * SUBLANES
HEAD_PAIR = LANES // ATTN_HEAD_DIM
N_PAIRS = N_ATTN_HEADS // HEAD_PAIR
MAX_DIL = max(DILATIONS)
ATTN_ROWS = 128
TOKEN_TILE = 1024
INPROJ_TILE = 1024
RET_CHUNK = 256
ATTN_PAIRS = 2
FF_CHUNK = 256
FF_STREAMS = 4
GELU_C1 = float(np.sqrt(2.0 / np.pi).astype(np.float32))
GELU_C3 = float(np.float32(GELU_C1 * 0.044715))
VMEM_LIMIT_BYTES = 56 * 1024 * 1024


def _pad_columns(w):
    return jnp.concatenate([w, jnp.zeros((w.shape[0], LANES), w.dtype)], axis=1)


def _dot_resident(lhs, w_ref):
    return jnp.dot(lhs, w_ref[:, :w_ref.shape[1] - LANES], preferred_element_type=F32)


def _rms(x, gain):
    return x * lax.rsqrt(jnp.mean(x * x, axis=-1, keepdims=True) + NORM_EPS) * gain


def _inproj_retention_kernel(x_ref, gain_ref, w_ref, cos_ref, sin_ref, dmat_ref, qdec_ref, kdec_ref, sdec_ref,
                             qa_ref, ka_ref, va_ref, ret_ref, state_ref, stage_ref):
    @pl.when(pl.program_id(1) == 0)
    def _():
        state_ref[...] = jnp.zeros_like(state_ref)

    base = 3 * ATTN_WIDTH
    vbase = base + 2 * RET_QK_WIDTH
    gbase = vbase + RET_V_WIDTH
    lane = lax.broadcasted_iota(jnp.int32, (1, RET_QK_WIDTH), 1)
    first_half = (lane % RET_QK_DIM) < (RET_QK_DIM // 2)
    lane128 = lax.broadcasted_iota(jnp.int32, (1, LANES), 1)
    low_lanes = lane128 < RET_QK_DIM
    row128 = lax.broadcasted_iota(jnp.int32, (LANES, 1), 0)
    low_rows = row128 < RET_QK_DIM
    nt = (((1,), (1,)), ((), ()))
    tn = (((0,), (0,)), ((), ()))
    view_rows = RET_CHUNK // MAX_DIL
    pitch = stage_ref.shape[1] // MAX_DIL

    for c in range(x_ref.shape[0] // RET_CHUNK):
        rows = slice(c * RET_CHUNK, (c + 1) * RET_CHUNK)
        h = _rms(x_ref[rows, :], gain_ref[...]).astype(BF16)
        proj = _dot_resident(h, w_ref)

        for t, out_ref in enumerate((qa_ref, ka_ref, va_ref)):
            for hp in range(N_PAIRS):
                col = t * N_PAIRS + hp
                block = proj[:, col * LANES:(col + 1) * LANES]
                for g in range(RET_CHUNK // SUBLANES):
                    m = c * view_rows + g * SUBLANES // MAX_DIL
                    r0 = g * SUBLANES % MAX_DIL
                    stage_ref[col, pl.ds(r0 * pitch + m, SUBLANES, stride=pitch), :] = (
                        block[g * SUBLANES:(g + 1) * SUBLANES])
                for r in range(MAX_DIL):
                    piece = stage_ref[col, r * pitch + c * view_rows:r * pitch + (c + 1) * view_rows, :]
                    out_ref[hp, c * view_rows:(c + 1) * view_rows, r * LANES:(r + 1) * LANES] = piece.astype(BF16)

        def rotary(t):
            swapped = jnp.where(first_half,
                                pltpu.roll(t, RET_QK_WIDTH - RET_QK_DIM // 2, axis=1),
                                pltpu.roll(t, RET_QK_DIM // 2, axis=1))
            cos = jnp.concatenate([cos_ref[rows, :]] * (RET_QK_WIDTH // LANES), axis=1)
            sin = jnp.concatenate([sin_ref[rows, :]] * (RET_QK_WIDTH // LANES), axis=1)
            return t * cos + swapped * sin

        qr = rotary(proj[:, base:base + RET_QK_WIDTH])
        kr = rotary(proj[:, base + RET_QK_WIDTH:base + 2 * RET_QK_WIDTH])

        for p in range(N_RET_HEADS // 2):
            cols = slice(p * LANES, (p + 1) * LANES)
            q2 = qr[:, cols]
            k2 = kr[:, cols]
            q_heads = (jnp.where(low_lanes, q2, 0.0), jnp.where(low_lanes, 0.0, q2))
            lhs = jnp.concatenate(q_heads, axis=0).astype(BF16)
            scores = lax.dot_general(lhs, k2.astype(BF16), nt, preferred_element_type=F32)
            kdec = (k2 * kdec_ref[:, cols]).astype(BF16)
            state = state_ref[p]
            state_bf = state.astype(BF16)
            kv = []
            for hh in range(2):
                head = 2 * p + hh
                v_h = proj[:, vbase + head * RET_V_DIM:vbase + (head + 1) * RET_V_DIM].astype(BF16)
                s_h = (scores[hh * RET_CHUNK:(hh + 1) * RET_CHUNK] * dmat_ref[head]).astype(BF16)
                inner = jnp.dot(s_h, v_h, preferred_element_type=F32)
                q_dec = (q_heads[hh] * qdec_ref[:, cols]).astype(BF16)
                cross = jnp.dot(q_dec, state_bf, preferred_element_type=F32)
                o = inner + cross
                o = o * lax.rsqrt(jnp.mean(o * o, axis=-1, keepdims=True) + NORM_EPS)
                g = proj[:, gbase + head * RET_V_DIM:gbase + (head + 1) * RET_V_DIM]
                o = o * (g * (1.0 / (1.0 + jnp.exp(-g))))
                ret_ref[rows, head * RET_V_DIM:(head + 1) * RET_V_DIM] = o.astype(BF16)
                kv.append(lax.dot_general(kdec, v_h, tn, preferred_element_type=F32))
            state_ref[p] = sdec_ref[p] * state + jnp.where(low_rows, kv[0], kv[1])


def _retention_constants(seq):
    freqs = ROPE_BASE ** (-np.arange(0, RET_QK_DIM, 2, dtype=np.float64) / RET_QK_DIM)
    ang = np.arange(seq, dtype=np.float64)[:, None] * freqs[None]
    cos_h = np.concatenate([np.cos(ang), np.cos(ang)], axis=1)
    sin_h = np.concatenate([-np.sin(ang), np.sin(ang)], axis=1)
    cos_t = np.tile(cos_h, (1, LANES // RET_QK_DIM)).astype(np.float32)
    sin_t = np.tile(sin_h, (1, LANES // RET_QK_DIM)).astype(np.float32)
    log_gamma = np.log1p(-np.exp2(-5.0 - np.arange(N_RET_HEADS, dtype=np.float64)))
    idx = np.arange(RET_CHUNK, dtype=np.float64)
    rel = idx[:, None] - idx[None, :]
    dmat = np.where(rel >= 0, np.exp(log_gamma[:, None, None] * np.maximum(rel, 0.0)), 0.0)
    qdec = np.exp(log_gamma[None, :] * (idx + 1.0)[:, None])
    kdec = np.exp(log_gamma[None, :] * (RET_CHUNK - 1 - idx)[:, None])
    qdec = np.repeat(qdec, RET_QK_DIM, axis=1)
    kdec = np.repeat(kdec, RET_QK_DIM, axis=1)
    sdec = np.repeat(np.exp(log_gamma * RET_CHUNK), RET_QK_DIM).reshape(N_RET_HEADS // 2, LANES, 1)
    sdec = np.broadcast_to(sdec, (N_RET_HEADS // 2, LANES, LANES))
    f = lambda a: jnp.asarray(np.ascontiguousarray(a), dtype=F32)
    return f(cos_t), f(sin_t), f(dmat), f(qdec), f(kdec), f(sdec)


def _inproj_retention(x, gain, w_in_bf):
    b, s, d = x.shape
    tm = INPROJ_TILE
    in_width = w_in_bf.shape[1]
    cos_t, sin_t, dmat, qdec, kdec, sdec = _retention_constants(s)
    const = lambda shape: pl.BlockSpec(shape, lambda bi, i: (0,) * len(shape))
    pair_spec = pl.BlockSpec((None, N_PAIRS, tm // MAX_DIL, MAX_DIL * LANES), lambda bi, i: (bi, 0, i, 0))
    pair_shape = jax.ShapeDtypeStruct((b, N_PAIRS, s // MAX_DIL, MAX_DIL * LANES), BF16)
    return pl.pallas_call(
        _inproj_retention_kernel,
        grid=(b, s // tm),
        in_specs=[
            pl.BlockSpec((None, tm, d), lambda bi, i: (bi, i, 0)),
            const((1, d)),
            const((d, in_width)),
            pl.BlockSpec((tm, LANES), lambda bi, i: (i, 0)),
            pl.BlockSpec((tm, LANES), lambda bi, i: (i, 0)),
            const((N_RET_HEADS, RET_CHUNK, RET_CHUNK)),
            const((RET_CHUNK, RET_QK_WIDTH)),
            const((RET_CHUNK, RET_QK_WIDTH)),
            const((N_RET_HEADS // 2, LANES, LANES)),
        ],
        out_specs=[pair_spec, pair_spec, pair_spec,
                   pl.BlockSpec((None, tm, RET_V_WIDTH), lambda bi, i: (bi, i, 0))],
        out_shape=[pair_shape, pair_shape, pair_shape, jax.ShapeDtypeStruct((b, s, RET_V_WIDTH), BF16)],
        scratch_shapes=[pltpu.VMEM((N_RET_HEADS // 2, LANES, LANES), F32),
                        pltpu.VMEM((3 * N_PAIRS, MAX_DIL * (tm // MAX_DIL + SUBLANES), LANES), F32)],
        compiler_params=pltpu.CompilerParams(dimension_semantics=("arbitrary", "arbitrary"),
                                             vmem_limit_bytes=VMEM_LIMIT_BYTES),
        name="inproj_retention",
    )(x, gain, w_in_bf, cos_t, sin_t, dmat, qdec, kdec, sdec)


def _attention_biases():
    q = np.arange(ATTN_ROWS)[None, :]
    k = np.arange(2 * ATTN_ROWS)[:, None]
    out = []
    for d in PATTERN_ORDER:
        g = MAX_DIL // d
        qn = ATTN_ROWS // g
        dist = g * (qn + q % qn - k % (2 * qn)) + q // qn - k // (2 * qn)
        out += [(dist, k % (2 * qn) >= qn)]
    biases = []
    for dist, in_seq in out:
        band = (dist >= 0) & (dist <= ATTN_STEPS)
        biases.append(np.where(band, 0.0, MASK_VALUE))
        biases.append(np.where(band & in_seq, 0.0, MASK_VALUE))
    return jnp.asarray(np.stack(biases), dtype=BF16)


def _attention_kernel(bias_ref, q_ref, k_ref, v_ref, o_ref, *scratch):
    first_tile = pl.program_id(2) == 0
    for hs in range(ATTN_PAIRS):
        _attention_pair(first_tile, bias_ref, q_ref.at[hs], k_ref.at[hs], v_ref.at[hs], o_ref.at[hs],
                        *(ref.at[hs] for ref in scratch))


def _attention_pair(first_tile, bias_ref, q_ref, k_ref, v_ref, o_ref, q0_s, q1_s, k_s, v_s, num_s, den_s, max_s,
                    qb_s, kb_s, vb_s):
    rows = ATTN_ROWS

    @pl.when(first_tile)
    def _():
        k_s[:, 0:rows, :] = jnp.zeros((MAX_DIL, rows, LANES), F32)
        v_s[:, 0:rows, :] = jnp.zeros((MAX_DIL, rows, LANES), F32)
        kb_s[...] = jnp.zeros_like(kb_s)
        vb_s[...] = jnp.zeros_like(vb_s)

    lane = lax.broadcasted_iota(jnp.int32, (1, LANES), 1)
    low = lane < ATTN_HEAD_DIM
    for r in range(MAX_DIL):
        cols = slice(r * LANES, (r + 1) * LANES)
        qf = q_ref[:, cols].astype(F32)
        for head, masked in enumerate((jnp.where(low, qf, 0.0), jnp.where(low, 0.0, qf))):
            (q0_s, q1_s)[head][r] = masked
            qb_s[head, r] = masked.astype(BF16)
        k_s[r, rows:2 * rows, :] = k_ref[:, cols].astype(F32)
        v_s[r, rows:2 * rows, :] = v_ref[:, cols].astype(F32)

    ones = jnp.ones((2 * rows, LANES), BF16)
    pick = (lax.broadcasted_iota(jnp.int32, (2 * rows, LANES), 0) % rows
            == lax.broadcasted_iota(jnp.int32, (2 * rows, LANES), 1))
    pick = jnp.where(pick, 1.0, 0.0).astype(BF16)
    nt = (((1,), (1,)), ((), ()))

    def attend(q2, k2, v2, bias):
        s = lax.dot_general(jnp.concatenate([q2, pick], axis=1), jnp.concatenate([k2, bias], axis=1), nt,
                            preferred_element_type=F32)
        m = jnp.max(s, axis=-1, keepdims=True)
        p = jnp.exp2(s - m).astype(BF16)
        pv = jnp.dot(p, jnp.concatenate([v2, ones], axis=1), preferred_element_type=F32)
        return (jnp.where(low, pv[:rows, :LANES], pv[rows:, :LANES]),
                jnp.where(low, pv[:rows, LANES:], pv[rows:, LANES:]),
                jnp.where(low, m[:rows], m[rows:]))

    def variant(cond):
        return jnp.where(cond, 1, 0)

    def rows_of(pieces, dtype=None):
        stacked = jnp.concatenate(pieces, axis=0)
        return stacked if dtype is None else stacked.astype(dtype)

    def window(hist_s, cur_ref, res, lo, hi):
        cols = slice(res * LANES, (res + 1) * LANES)
        parts = []
        if lo < rows:
            parts.append(hist_s[res, lo:min(hi, rows), :])
        if hi > rows:
            parts.append(cur_ref[max(lo, rows) - rows:hi - rows, cols])
        return parts

    def store(slot, pieces, results):
        for res, rows_dst, rows_src in pieces:
            for ref, val in zip((num_s, den_s, max_s), results):
                ref[slot, res, rows_dst, :] = val[rows_src]

    def unit(d, c, j):
        g = MAX_DIL // d
        qn = rows // g
        residues = [a * d + c for a in range(g)]
        qrow, klo, khi = slice(qn * j, qn * (j + 1)), rows - qn + qn * j, rows + qn + qn * j
        table = 2 * PATTERN_ORDER.index(d)
        bias = bias_ref[table + variant(first_tile)] if j == 0 else bias_ref[table]
        if qn % BF16_ROWS == 0:
            q2 = rows_of([qb_s[h, res, qrow, :] for h in range(HEAD_PAIR) for res in residues])
            k2 = rows_of([p for res in residues for p in window(kb_s, k_ref, res, klo, khi)])
            v2 = rows_of([p for res in residues for p in window(vb_s, v_ref, res, klo, khi)])
        else:
            q2 = rows_of([q_s[res, qrow, :] for q_s in (q0_s, q1_s) for res in residues], BF16)
            k2 = rows_of([k_s[res, klo:khi, :] for res in residues], BF16)
            v2 = rows_of([v_s[res, klo:khi, :] for res in residues], BF16)
        return attend(q2, k2, v2, bias), [(res, qrow, slice(qn * a, qn * (a + 1))) for a, res in enumerate(residues)]

    for slot, d in enumerate(PATTERN_ORDER[1:]):
        for c in range(d):
            for j in range(MAX_DIL // d):
                results, pieces = unit(d, c, j)
                store(slot, pieces, results)

    for r in range(MAX_DIL):
        (n0, d0, m0), _ = unit(MAX_DIL, r, 0)
        m1, m2 = max_s[0, r], max_s[1, r]
        top = jnp.maximum(jnp.maximum(m0, m1), m2)
        e0, e1, e2 = jnp.exp2(m0 - top), jnp.exp2(m1 - top), jnp.exp2(m2 - top)
        num = e0 * n0 + e1 * num_s[0, r] + e2 * num_s[1, r]
        den = e0 * d0 + e1 * den_s[0, r] + e2 * den_s[1, r]
        o_ref[:, r * LANES:(r + 1) * LANES] = (num * (1.0 / den)).astype(BF16)
        k_s[r, 0:rows, :] = k_s[r, rows:2 * rows, :]
        v_s[r, 0:rows, :] = v_s[r, rows:2 * rows, :]
        kb_s[r] = k_ref[:, r * LANES:(r + 1) * LANES]
        vb_s[r] = v_ref[:, r * LANES:(r + 1) * LANES]


def _dilated_attention(qa, ka, va):
    b, n_pairs, view_len, width = qa.shape
    tile = pl.BlockSpec((None, ATTN_PAIRS, ATTN_ROWS, width), lambda bi, hp, i: (bi, hp, i, 0))
    biases = _attention_biases()
    scratch = lambda n: pltpu.VMEM((ATTN_PAIRS, MAX_DIL, n, LANES), F32)
    return pl.pallas_call(
        _attention_kernel,
        grid=(b, n_pairs // ATTN_PAIRS, view_len // ATTN_ROWS),
        in_specs=[pl.BlockSpec(biases.shape, lambda bi, hp, i: (0, 0, 0)), tile, tile, tile],
        out_specs=tile,
        out_shape=jax.ShapeDtypeStruct(qa.shape, BF16),
        scratch_shapes=[scratch(ATTN_ROWS), scratch(ATTN_ROWS), scratch(2 * ATTN_ROWS), scratch(2 * ATTN_ROWS)]
        + [pltpu.VMEM((ATTN_PAIRS, len(DILATIONS) - 1, MAX_DIL, ATTN_ROWS, LANES), F32)] * 3
        + [pltpu.VMEM((ATTN_PAIRS, HEAD_PAIR, MAX_DIL, ATTN_ROWS, LANES), BF16)]
        + [pltpu.VMEM((ATTN_PAIRS, MAX_DIL, ATTN_ROWS, LANES), BF16)] * 2,
        compiler_params=pltpu.CompilerParams(dimension_semantics=("arbitrary", "arbitrary", "arbitrary"),
                                             vmem_limit_bytes=VMEM_LIMIT_BYTES),
        name="dilated_attention",
    )(biases, qa, ka, va)


def _outproj_ffn_kernel(x_ref, attn_ref, ret_ref, gpost_ref, gpre_ref, gffn_ref, wout_ref, wup_ref, cw_ref, cb_ref,
                        wdown_ref, o_ref, act_s, halo_s, stage_s):
    tm = x_ref.shape[0]
    d_ff = wdown_ref.shape[0]

    @pl.when(pl.program_id(1) == 0)
    def _():
        halo_s[...] = jnp.zeros_like(halo_s)

    first_rows = lax.broadcasted_iota(jnp.int32, (SUBLANES, 1), 0)
    hm = tm // FF_STREAMS
    vm = hm // MAX_DIL
    x1, hn = [], []
    for st in range(FF_STREAMS):
        rows = slice(st * hm, (st + 1) * hm)
        for hp in range(N_PAIRS):
            for r in range(MAX_DIL):
                piece = attn_ref[hp, st * vm:(st + 1) * vm, r * LANES:(r + 1) * LANES].astype(F32)
                stage_s[hp, pl.ds(st * hm + r, vm, stride=MAX_DIL), :] = piece
        mixed = jnp.concatenate([stage_s[hp, rows, :].astype(BF16) for hp in range(N_PAIRS)] + [ret_ref[rows, :]],
                                axis=-1)
        mix = _dot_resident(mixed, wout_ref)
        x1.append(x_ref[rows, :] + _rms(mix, gpost_ref[...]))
        hn.append(_rms(x1[st], gpre_ref[...]).astype(BF16))

    def conv(cols, u):
        halo = halo_s[:, cols]
        halo_s[:, cols] = u[hm - SUBLANES:hm]
        taps = []
        for shift in (1, 2):
            rolled = pltpu.roll(u, shift, axis=0)
            head = jnp.where(first_rows < shift, pltpu.roll(halo, shift, axis=0), rolled[0:SUBLANES])
            taps.append(jnp.concatenate([head, rolled[SUBLANES:]], axis=0))
        return cb_ref[:, cols] + cw_ref[2:3, cols] * u + cw_ref[1:2, cols] * taps[0] + cw_ref[0:1, cols] * taps[1]

    for c in range(d_ff // FF_CHUNK):
        gcols = slice(c * FF_CHUNK, (c + 1) * FF_CHUNK)
        ucols = slice(d_ff + c * FF_CHUNK, d_ff + (c + 1) * FF_CHUNK)
        for st in range(FF_STREAMS):
            gate = conv(gcols, jnp.dot(hn[st], wup_ref[:, gcols], preferred_element_type=F32))
            half_up = conv(ucols, jnp.dot(hn[st], wup_ref[:, ucols], preferred_element_type=F32))
            inner = gate * (GELU_C1 + GELU_C3 * (gate * gate))
            act_s[st * hm:(st + 1) * hm, gcols] = (gate * (1.0 + jnp.tanh(inner)) * half_up).astype(BF16)

    for st in range(FF_STREAMS):
        rows = slice(st * hm, (st + 1) * hm)
        ffn = _dot_resident(act_s[rows, :], wdown_ref)
        o_ref[rows, :] = x1[st] + _rms(ffn, gffn_ref[...])


def _outproj_ffn(x, attn, ret, g_post, g_pre, g_ffn, w_out_bf, w_up_bf, conv_w, conv_b, w_down_bf):
    b, s, d = x.shape
    tm = TOKEN_TILE
    d_ff = w_down_bf.shape[0]
    const = lambda shape: pl.BlockSpec(shape, lambda bi, i: (0,) * len(shape), pipeline_mode=pl.Buffered(1))
    return pl.pallas_call(
        _outproj_ffn_kernel,
        grid=(b, s // tm),
        in_specs=[
            pl.BlockSpec((None, tm, d), lambda bi, i: (bi, i, 0)),
            pl.BlockSpec((None, N_PAIRS, tm // MAX_DIL, MAX_DIL * LANES), lambda bi, i: (bi, 0, i, 0)),
            pl.BlockSpec((None, tm, RET_V_WIDTH), lambda bi, i: (bi, i, 0)),
            const((1, d)), const((1, d)), const((1, d)),
            const(w_out_bf.shape), const(w_up_bf.shape), const(conv_w.shape), const(conv_b.shape),
            const(w_down_bf.shape),
        ],
        out_specs=pl.BlockSpec((None, tm, d), lambda bi, i: (bi, i, 0)),
        out_shape=jax.ShapeDtypeStruct((b, s, d), F32),
        scratch_shapes=[pltpu.VMEM((tm, d_ff), BF16),
                        pltpu.VMEM((SUBLANES, 2 * d_ff), F32),
                        pltpu.VMEM((N_PAIRS, tm, LANES), F32)],
        compiler_params=pltpu.CompilerParams(dimension_semantics=("arbitrary", "arbitrary"),
                                             vmem_limit_bytes=VMEM_LIMIT_BYTES),
        name="outproj_ffn",
    )(x, attn, ret, g_post, g_pre, g_ffn, w_out_bf, w_up_bf, conv_w, conv_b, w_down_bf)


def _layer(x, g_mix_pre, g_mix_post, w_in, w_out, g_ffn_pre, g_ffn_post, w_up, conv_w, conv_b, w_down):
    d = x.shape[-1]
    d_ff = w_down.shape[0]
    col_scale = np.ones((w_in.shape[1],), np.float32)
    col_scale[:ATTN_WIDTH] = ATTN_HEAD_DIM ** -0.5 * np.log2(np.e)
    kr0 = 3 * ATTN_WIDTH + RET_QK_WIDTH
    col_scale[kr0:kr0 + RET_QK_WIDTH] = RET_QK_DIM ** -0.5
    w_in_bf = _pad_columns((w_in * col_scale).astype(BF16))
    qa, ka, va, ret = _inproj_retention(x, g_mix_pre.reshape(1, d), w_in_bf)
    attn = _dilated_attention(qa, ka, va)
    up_half = np.concatenate([np.ones((d_ff,), np.float32), np.full((d_ff,), 0.5, np.float32)])
    return _outproj_ffn(x, attn, ret, g_mix_post.reshape(1, d), g_ffn_pre.reshape(1, d), g_ffn_post.reshape(1, d),
                        _pad_columns(w_out.astype(BF16)), w_up.astype(BF16), conv_w * up_half,
                        conv_b.reshape(1, -1) * up_half, _pad_columns(w_down.astype(BF16)))


def kernel(x, mix_pre_gain, mix_post_gain, w_in, w_out, ffn_pre_gain, ffn_post_gain, w_up, conv_w, conv_b, w_down):
    for layer in range(mix_pre_gain.shape[0]):
        x = _layer(x, mix_pre_gain[layer], mix_post_gain[layer], w_in[layer], w_out[layer], ffn_pre_gain[layer],
                   ffn_post_gain[layer], w_up[layer], conv_w[layer], conv_b[layer], w_down[layer])
    return x
```
